```python
import jax, jax.numpy as jnp
from jax import lax
import numpy as np

D_MODEL = 2048
BATCH = 4
SEQ = 2048
DEPTH = 2
DEC_BATCH = 32
DEC_SEQ = 8
PAST_LEN = 8192
PAGE_SIZE = 128

HGRN_EXPAND = 128
H_A = D_MODEL // HGRN_EXPAND
DK = HGRN_EXPAND
DV = D_MODEL // H_A
HGRN_CHUNK = 16
HEAD_DIM = 128
N_HEADS_B = D_MODEL // HEAD_DIM
KV_HEADS = N_HEADS_B
MOBA_BLOCK = 256
MOBA_TOPK = 3
Q_CHUNK = 16
FFN_HIDDEN = -(-(8 * D_MODEL) // (3 * 256)) * 256
N_A_LAYERS = DEPTH // 2
N_B_LAYERS = DEPTH - N_A_LAYERS
LN_EPS = 1e-5
RMS_EPS = 1e-6

kernel_name = "yoco_hgrn2_moba_decoder_step"

F32 = jnp.float32


def _layer_norm(x, g, b):
    xf = x.astype(F32)
    mu = jnp.mean(xf, axis=-1, keepdims=True)
    var = jnp.mean(jnp.square(xf - mu), axis=-1, keepdims=True)
    return ((xf - mu) * lax.rsqrt(var + LN_EPS) * g.astype(F32) + b.astype(F32)).astype(x.dtype)


def _ada(c, w, b, n):
    m = jax.nn.silu(c) @ w + b
    return jnp.split(m[:, None, :], n, axis=-1)


def _swiglu(h, w_in, w_out):
    a, u = jnp.split(h @ w_in, 2, axis=-1)
    return (jax.nn.silu(a) * u) @ w_out


def _hgrn_scan(q, log_f, k, v, s0):
    B, T, H, dk = q.shape
    dv = v.shape[-1]
    n = -(-T // HGRN_CHUNK)
    pad = n * HGRN_CHUNK - T
    if pad:
        pw = ((0, 0), (0, pad), (0, 0), (0, 0))
        q, log_f, k, v = (jnp.pad(a, pw) for a in (q, log_f, k, v))

    def blocks(a):
        return a.reshape(B, n, HGRN_CHUNK, H, a.shape[-1]).transpose(1, 0, 3, 2, 4)

    causal = jnp.tril(jnp.ones((HGRN_CHUNK, HGRN_CHUNK), bool))[:, :, None]

    def step(S, inp):
        qc, lfc, kc, vc = inp
        b = jnp.cumsum(lfc, axis=2)
        diff = b[:, :, :, None, :] - b[:, :, None, :, :]
        decay = jnp.exp(jnp.where(causal, diff, -jnp.inf))
        attn = jnp.einsum('bhtd,bhsd,bhtsd->bhts', qc, kc, decay)
        o = (jnp.einsum('bhtd,bhde->bhte', qc * jnp.exp(b), S)
             + jnp.einsum('bhts,bhse->bhte', attn, vc))
        b_last = b[:, :, -1:, :]
        S = (jnp.exp(b_last[:, :, 0, :])[..., None] * S
             + jnp.einsum('bhsd,bhse->bhde', kc * jnp.exp(b_last - b), vc))
        return S, o

    S, o = lax.scan(step, s0, (blocks(q), blocks(log_f), blocks(k), blocks(v)))
    o = o.transpose(1, 0, 3, 2, 4).reshape(B, n * HGRN_CHUNK, H, dv)[:, :T]
    return o, S


def _hgrn_mixer(h, s0, w_in, lb, norm_g, w_out):
    B, T, _ = h.shape
    q, f_logit, i_in, g = jnp.split(h @ w_in, 4, axis=-1)
    lbh = lb.reshape(H_A, DK)
    f = lbh + (1.0 - lbh) * jax.nn.sigmoid(f_logit.reshape(B, T, H_A, DK).astype(F32))
    log_f = jnp.log(f)
    k = 1.0 - f
    qf = q.reshape(B, T, H_A, DK).astype(F32)
    vf = i_in.reshape(B, T, H_A, DV).astype(F32)
    o, s_fin = _hgrn_scan(qf, log_f, k, vf, s0.astype(F32))
    o = o * lax.rsqrt(jnp.mean(jnp.square(o), axis=-1, keepdims=True) + RMS_EPS) * norm_g.astype(F32)
    o = o.reshape(B, T, H_A * DV).astype(h.dtype) * jax.nn.silu(g)
    return o @ w_out, s_fin.astype(s0.dtype)


def _moba_single(q, k, v):
    T, H, dh = q.shape
    L = k.shape[0]
    pos0 = L - T
    qc_len = min(Q_CHUNK, T)
    nq = -(-T // qc_len)
    t_pad = nq * qc_len
    nb = -(-(pos0 + t_pad) // MOBA_BLOCK)
    q = jnp.pad(q, ((0, t_pad - T), (0, 0), (0, 0)))
    kpad = ((0, nb * MOBA_BLOCK - L), (0, 0), (0, 0))
    kblk = jnp.pad(k, kpad).reshape(nb, MOBA_BLOCK, H, dh)
    vblk = jnp.pad(v, kpad).reshape(nb, MOBA_BLOCK, H, dh)
    k_mean = jnp.mean(kblk.astype(F32), axis=1)
    n_sel = min(MOBA_TOPK, nb)
    heads = jnp.arange(H)
    blk_ids = jnp.arange(nb)
    offs = jnp.arange(MOBA_BLOCK)

    def chunk(args):
        qc, pc = args
        ob = pc // MOBA_BLOCK
        gate = jnp.einsum('qhd,nhd->qhn', qc.astype(F32), k_mean)
        gate = jnp.where(blk_ids[None, None, :] < ob[:, None, None], gate, -jnp.inf)
        _, idx = lax.top_k(gate, n_sel)
        valid = idx < ob[:, None, None]
        k_sel = kblk[idx, :, heads[None, :, None]]
        v_sel = vblk[idx, :, heads[None, :, None]]
        k_own = kblk[ob[:, None], :, heads[None, :]]
        v_own = vblk[ob[:, None], :, heads[None, :]]
        s_sel = jnp.einsum('qhd,qhjkd->qhjk', qc, k_sel).astype(F32)
        s_sel = jnp.where(valid[..., None], s_sel, -jnp.inf).reshape(qc_len, H, n_sel * MOBA_BLOCK)
        s_own = jnp.einsum('qhd,qhkd->qhk', qc, k_own).astype(F32)
        own_pos = ob[:, None] * MOBA_BLOCK + offs[None, :]
        s_own = jnp.where((own_pos <= pc[:, None])[:, None, :], s_own, -jnp.inf)
        p = jax.nn.softmax(jnp.concatenate([s_sel, s_own], axis=-1), axis=-1).astype(v_sel.dtype)
        p_sel = p[..., :n_sel * MOBA_BLOCK].reshape(qc_len, H, n_sel, MOBA_BLOCK)
        p_own = p[..., n_sel * MOBA_BLOCK:]
        return (jnp.einsum('qhjk,qhjkd->qhd', p_sel, v_sel)
                + jnp.einsum('qhk,qhkd->qhd', p_own, v_own))

    pos = pos0 + jnp.arange(t_pad, dtype=jnp.int32)
    o = lax.map(chunk, (q.reshape(nq, qc_len, H, dh), pos.reshape(nq, qc_len)))
    return o.reshape(t_pad, H, dh)[:T]


def setup_inputs(seed: int = 0) -> dict:
    key = jax.random.key(seed)
    ks = jax.random.split(key, 24)
    D = D_MODEL
    beta = (8.0 * DEPTH) ** -0.25
    n_pages = PAST_LEN // PAGE_SIZE
    n_used = DEC_BATCH * n_pages
    n_phys = n_used + max(1, n_used // 4)
    nrm = jax.random.normal
    kv_cols = KV_HEADS * HEAD_DIM
    kv_scale = jnp.concatenate([jnp.ones((kv_cols,), F32), beta * jnp.ones((kv_cols,), F32)])
    page_table = jax.random.permutation(ks[7], n_phys)[:n_used].reshape(DEC_BATCH, n_pages).astype(jnp.int32)
    return {
        "x_prompt": nrm(ks[0], (BATCH, SEQ, D), F32),
        "x_sample": nrm(ks[1], (DEC_BATCH, DEC_SEQ, D), F32),
        "state_hgrn": 0.5 * nrm(ks[2], (N_A_LAYERS, DEC_BATCH, H_A, DK, DV), F32),
        "cache_k": nrm(ks[3], (n_phys, PAGE_SIZE, KV_HEADS, HEAD_DIM), F32),
        "cache_v": nrm(ks[4], (n_phys, PAGE_SIZE, KV_HEADS, HEAD_DIM), F32),
        "page_table": page_table,
        "c_prompt": nrm(ks[5], (BATCH, D), F32),
        "c_sample": nrm(ks[6], (DEC_BATCH, D), F32),
        "a_w_in": nrm(ks[8], (N_A_LAYERS, D, 2 * H_A * DK + H_A * DV + D), F32) * D ** -0.5,
        "a_lb_logits": 0.5 * nrm(ks[9], (N_A_LAYERS + 1, H_A * DK), F32),
        "a_norm_g": 1.0 + 0.02 * nrm(ks[10], (N_A_LAYERS, H_A, DV), F32),
        "a_w_out": nrm(ks[11], (N_A_LAYERS, H_A * DV, D), F32) * (H_A * DV) ** -0.5 * beta,
        "b_w_q": nrm(ks[12], (N_B_LAYERS, D, N_HEADS_B * HEAD_DIM), F32) * D ** -0.5,
        "b_w_o": nrm(ks[13], (N_B_LAYERS, N_HEADS_B * HEAD_DIM, D), F32) * (N_HEADS_B * HEAD_DIM) ** -0.5 * beta,
        "w_ada_kv": nrm(ks[14], (D, 2 * D), F32) * 0.3 * D ** -0.5,
        "b_ada_kv": 0.02 * nrm(ks[15], (2 * D,), F32),
        "w_kv": nrm(ks[16], (D, 2 * kv_cols), F32) * D ** -0.5 * kv_scale,
        "w_ada": nrm(ks[17], (DEPTH, D, 6 * D), F32) * 0.3 * D ** -0.5,
        "b_ada": 0.02 * nrm(ks[18], (DEPTH, 6 * D), F32),
        "ln_g": 1.0 + 0.02 * nrm(ks[19], (DEPTH, 2, D), F32),
        "ln_b": 0.02 * nrm(ks[20], (DEPTH, 2, D), F32),
        "ffn_w_in": nrm(ks[21], (DEPTH, D, 2 * FFN_HIDDEN), F32) * D ** -0.5,
        "ffn_w_out": nrm(ks[22], (DEPTH, FFN_HIDDEN, D), F32) * FFN_HIDDEN ** -0.5 * beta,
    }


def reference(x_prompt, x_sample, state_hgrn, cache_k, cache_v, page_table, c_prompt, c_sample,
              a_w_in, a_lb_logits, a_norm_g, a_w_out, b_w_q, b_w_o, w_ada_kv, b_ada_kv, w_kv,
              w_ada, b_ada, ln_g, ln_b, ffn_w_in, ffn_w_out):
    alpha = (2.0 * DEPTH) ** 0.25
    n_a = a_w_in.shape[0]
    lb_all = jnp.cumsum(jax.nn.softmax(a_lb_logits.astype(F32), axis=0), axis=0)

    def shared_kv(x, c):
        shift, scale = _ada(c, w_ada_kv, b_ada_kv, 2)
        z = x * (1.0 + scale) + shift
        k, v = jnp.split(z @ w_kv, 2, axis=-1)
        B, T, _ = x.shape
        return k.reshape(B, T, KV_HEADS, HEAD_DIM), v.reshape(B, T, KV_HEADS, HEAD_DIM)

    def trunk(x, c, s0, attend):
        B, T, _ = x.shape
        s_fin = []
        k_sh = v_sh = None
        for l in range(DEPTH):
            sh1, sc1, g1, sh2, sc2, g2 = _ada(c, w_ada[l], b_ada[l], 6)
            h = x * (1.0 + sc1) + sh1
            if l < n_a:
                mix, sf = _hgrn_mixer(h, s0[l], a_w_in[l], lb_all[l], a_norm_g[l], a_w_out[l])
                s_fin.append(sf)
            else:
                if l == n_a:
                    k_sh, v_sh = shared_kv(x, c)
                j = l - n_a
                q = (h @ b_w_q[j]).reshape(B, T, N_HEADS_B, HEAD_DIM) * (HEAD_DIM ** -0.5)
                o = attend(q, k_sh, v_sh)
                mix = o.reshape(B, T, N_HEADS_B * HEAD_DIM) @ b_w_o[j]
            x = _layer_norm(alpha * x + (1.0 + g1) * mix, ln_g[l, 0], ln_b[l, 0])
            h = x * (1.0 + sc2) + sh2
            x = _layer_norm(alpha * x + (1.0 + g2) * _swiglu(h, ffn_w_in[l], ffn_w_out[l]),
                            ln_g[l, 1], ln_b[l, 1])
        return x, jnp.stack(s_fin), k_sh, v_sh

    def attend_prompt(q, k, v):
        return lax.map(lambda a: _moba_single(a[0], a[1], a[2]), (q, k, v))

    past_len = page_table.shape[1] * cache_k.shape[1]

    def attend_sample(q, k, v):
        def one(a):
            qb, kb, vb, pt = a
            kp = cache_k[pt].reshape(past_len, KV_HEADS, HEAD_DIM)
            vp = cache_v[pt].reshape(past_len, KV_HEADS, HEAD_DIM)
            k_all = jnp.concatenate([kp, kb.astype(kp.dtype)], axis=0)
            v_all = jnp.concatenate([vp, vb.astype(vp.dtype)], axis=0)
            return _moba_single(qb, k_all, v_all)
        return lax.map(one, (q, k, v, page_table))

    s0_prompt = jnp.zeros((n_a, x_prompt.shape[0], H_A, DK, DV), state_hgrn.dtype)
    y_prompt, s_prompt, k_prompt, v_prompt = trunk(x_prompt, c_prompt, s0_prompt, attend_prompt)
    y_sample, s_sample, k_sample, v_sample = trunk(x_sample, c_sample, state_hgrn, attend_sample)
    return (y_prompt, y_sample, s_prompt, s_sample, k_prompt, v_prompt, k_sample, v_sample)
```

```python
import functools

import numpy as np
import jax
import jax.numpy as jnp
from jax import lax
from jax.experimental import pallas as pl
from jax.experimental.pallas import tpu as pltpu

F32 = jnp.float32
BF16 = jnp.bfloat16

D_MODEL = 2048
DEPTH = 2
HEAD = 128
N_HEADS = D_MODEL // HEAD
HGRN_PAD_CHUNK = 16
MOBA_BLOCK = 256
MOBA_TOPK = 3
PAGE_SIZE = 128
PAGES_PER_BLOCK = MOBA_BLOCK // PAGE_SIZE
LN_EPS = 1e-5
RMS_EPS = 1e-6
ALPHA = (2.0 * DEPTH) ** 0.25
VMEM_LIMIT_BYTES = 56 * 1024 * 1024

NEG_INF = float("-inf")


def _params():
    return pltpu.CompilerParams(vmem_limit_bytes=VMEM_LIMIT_BYTES)


def _dot(a, b):
    return lax.dot_general(a, b, (((1,), (0,)), ((), ())), preferred_element_type=F32)


def _dot_nt(a, b, precision=None):
    return lax.dot_general(a, b, (((1,), (1,)), ((), ())), preferred_element_type=F32,
                           precision=precision)


def _dot_tn(a, b):
    return lax.dot_general(a, b, (((0,), (0,)), ((), ())), preferred_element_type=F32)


def _silu(x):
    return x * jax.nn.sigmoid(x)


def _lb_body(logit_ref, out_ref):
    z = logit_ref[...]
    n = z.shape[0]
    m = z[0:1]
    for r in range(1, n):
        m = jnp.maximum(m, z[r:r + 1])
    e = jnp.exp(z - m)
    tot = e[0:1]
    for r in range(1, n):
        tot = tot + e[r:r + 1]
    p = e / tot
    run = p[0:1]
    out_ref[0:1, :] = run
    for r in range(1, n):
        run = run + p[r:r + 1]
        out_ref[r:r + 1, :] = run


def _lower_bounds(logits):
    return pl.pallas_call(
        _lb_body, out_shape=jax.ShapeDtypeStruct(logits.shape, F32), name="lower_bounds")(logits)


def _mm_body(*refs, silu_in, scale, has_bias):
    if has_bias:
        x_ref, w_ref, b_ref, o_ref, wb_ref = refs
    else:
        x_ref, w_ref, o_ref, wb_ref = refs

    @pl.when(pl.program_id(1) == 0)
    def _():
        wb_ref[...] = w_ref[...].astype(BF16)

    x = x_ref[...]
    if silu_in:
        x = _silu(x)
    acc = _dot(x.astype(BF16), wb_ref[...])
    if has_bias:
        acc = acc + b_ref[...]
    if scale is not None:
        acc = acc * scale
    o_ref[...] = acc.astype(o_ref.dtype)


def _matmul(x, w, *, bias=None, silu_in=False, scale=None, tm, tn, name):
    m, k = x.shape
    n = w.shape[1]
    in_specs = [pl.BlockSpec((tm, k), lambda j, i: (i, 0)),
                pl.BlockSpec((k, tn), lambda j, i: (0, j))]
    args = [x, w]
    if bias is not None:
        in_specs.append(pl.BlockSpec((1, tn), lambda j, i: (0, j)))
        args.append(bias.reshape(1, n))
    return pl.pallas_call(
        functools.partial(_mm_body, silu_in=silu_in, scale=scale, has_bias=bias is not None),
        grid=(n // tn, m // tm),
        in_specs=in_specs,
        out_specs=pl.BlockSpec((tm, tn), lambda j, i: (i, j)),
        out_shape=jax.ShapeDtypeStruct((m, n), F32),
        scratch_shapes=[pltpu.VMEM((k, tn), BF16)],
        compiler_params=_params(),
        name=name,
    )(*args)


def _swiglu_body(x_ref, wa_ref, wu_ref, o_ref, wab_ref, wub_ref):
    @pl.when(pl.program_id(1) == 0)
    def _():
        wab_ref[...] = wa_ref[...].astype(BF16)
        wub_ref[...] = wu_ref[...].astype(BF16)

    x = x_ref[...]
    a = _dot(x, wab_ref[...])
    u = _dot(x, wub_ref[...])
    o_ref[...] = (_silu(a) * u).astype(o_ref.dtype)


def _swiglu_in(x, w_in, *, tm, tn):
    m, k = x.shape
    f = w_in.shape[1] // 2
    nj = f // tn
    return pl.pallas_call(
        _swiglu_body,
        grid=(nj, m // tm),
        in_specs=[pl.BlockSpec((tm, k), lambda j, i: (i, 0)),
                  pl.BlockSpec((k, tn), lambda j, i: (0, j)),
                  pl.BlockSpec((k, tn), lambda j, i: (0, j + nj))],
        out_specs=pl.BlockSpec((tm, tn), lambda j, i: (i, j)),
        out_shape=jax.ShapeDtypeStruct((m, f), BF16),
        scratch_shapes=[pltpu.VMEM((k, tn), BF16), pltpu.VMEM((k, tn), BF16)],
        compiler_params=_params(),
        name="swiglu_in",
    )(x, w_in, w_in)


def _modulate_body(x_ref, sc_ref, sh_ref, o_ref):
    o_ref[...] = (x_ref[...] * (1.0 + sc_ref[...]) + sh_ref[...]).astype(o_ref.dtype)


def _mod_spec(vec, tm, rows_per_group):
    r = vec.shape[1]
    tiles = rows_per_group // tm
    return pl.BlockSpec((None, r, D_MODEL), lambda i, *_: (i // tiles, 0, 0))


def _modulate(x, sc, sh, *, tm, rows_per_group):
    m = x.shape[0]
    return pl.pallas_call(
        _modulate_body,
        grid=(m // tm,),
        in_specs=[pl.BlockSpec((tm, D_MODEL), lambda i: (i, 0)),
                  _mod_spec(sc, tm, rows_per_group), _mod_spec(sh, tm, rows_per_group)],
        out_specs=pl.BlockSpec((tm, D_MODEL), lambda i: (i, 0)),
        out_shape=jax.ShapeDtypeStruct((m, D_MODEL), BF16),
        compiler_params=_params(),
        name="modulate",
    )(x, sc, sh)


def _mm_ln_body(*refs, n_mod, nk):
    a_ref, w_ref, x_ref, gate_ref, lng_ref, lnb_ref = refs[:6]
    mod_refs = refs[6:6 + 2 * n_mod]
    y_ref = refs[6 + 2 * n_mod]
    h_refs = refs[7 + 2 * n_mod:7 + 3 * n_mod]
    acc_ref = refs[7 + 3 * n_mod]
    kk = pl.program_id(1)

    @pl.when(kk == 0)
    def _():
        acc_ref[...] = jnp.zeros_like(acc_ref)

    acc_ref[...] += _dot(a_ref[...].astype(BF16), w_ref[...].astype(BF16))

    @pl.when(kk == nk - 1)
    def _():
        r = ALPHA * x_ref[...] + (1.0 + gate_ref[...]) * acc_ref[...]
        mu = jnp.mean(r, axis=-1, keepdims=True)
        d = r - mu
        var = jnp.mean(d * d, axis=-1, keepdims=True)
        y = d * lax.rsqrt(var + LN_EPS) * lng_ref[...] + lnb_ref[...]
        y_ref[...] = y
        for j in range(n_mod):
            sc_ref, sh_ref = mod_refs[2 * j], mod_refs[2 * j + 1]
            h_refs[j][...] = (y * (1.0 + sc_ref[...]) + sh_ref[...]).astype(BF16)


def _matmul_ln(a, w, x, gate, ln_g, ln_b, mods, *, tm, tk, rows_per_group, name):
    m, k = a.shape
    nk = k // tk
    n_mod = len(mods)
    row_spec = pl.BlockSpec((tm, D_MODEL), lambda i, kk: (i, 0))
    vec_spec = pl.BlockSpec((1, D_MODEL), lambda i, kk: (0, 0))
    in_specs = [pl.BlockSpec((tm, tk), lambda i, kk: (i, kk)),
                pl.BlockSpec((tk, D_MODEL), lambda i, kk: (kk, 0)),
                row_spec, _mod_spec(gate, tm, rows_per_group), vec_spec, vec_spec]
    args = [a, w, x, gate, ln_g.reshape(1, D_MODEL), ln_b.reshape(1, D_MODEL)]
    for sc, sh in mods:
        in_specs += [_mod_spec(sc, tm, rows_per_group), _mod_spec(sh, tm, rows_per_group)]
        args += [sc, sh]
    out = pl.pallas_call(
        functools.partial(_mm_ln_body, n_mod=n_mod, nk=nk),
        grid=(m // tm, nk),
        in_specs=in_specs,
        out_specs=[row_spec] * (1 + n_mod),
        out_shape=[jax.ShapeDtypeStruct((m, D_MODEL), F32)]
        + [jax.ShapeDtypeStruct((m, D_MODEL), BF16)] * n_mod,
        scratch_shapes=[pltpu.VMEM((tm, D_MODEL), F32)],
        compiler_params=_params(),
        name=name,
    )(*args)
    return out[0], list(out[1:])


def _hgrn_tables(chunk):
    levels = chunk.bit_length() - 1
    t = np.arange(chunk)[:, None]
    r = np.arange(chunk)[None, :]
    mats = [(r <= t), (r > t)]
    for lv in range(levels):
        size = 1 << lv
        odd = ((t >> lv) & 1) == 1
        start = t & ~(size - 1)
        end = t | (size - 1)
        mats.append(np.where(odd, (r >= start) & (r <= t), (r > t) & (r <= end)))
    return np.concatenate(mats, axis=0).astype(np.float32), levels


def _hgrn_body(q_ref, f_ref, i_ref, g_ref, lb_ref, ng_ref, s0_ref, tab_ref,
               o_ref, sfin_ref, st_ref, *, chunk, valid, levels, n_chunks):
    c = pl.program_id(2)

    @pl.when(c == 0)
    def _():
        st_ref[...] = s0_ref[...].T

    lb = lb_ref[...]
    f = lb + (1.0 - lb) * jax.nn.sigmoid(f_ref[...])
    lf = jnp.log(f)
    k = 1.0 - f
    q = q_ref[...]
    v = i_ref[...]
    if valid < chunk:
        zeros = jnp.zeros((chunk - valid, HEAD), F32)
        lf, k, q, v = (jnp.concatenate([a, zeros], axis=0) for a in (lf, k, q, v))

    hi = lf.astype(BF16)
    r1 = lf - hi.astype(F32)
    mid = r1.astype(BF16)
    lo = (r1 - mid.astype(F32)).astype(BF16)
    tab = tab_ref[...]
    e_all = _dot(tab, hi) + _dot(tab, mid) + _dot(tab, lo)

    b = e_all[0:chunk]
    suf = e_all[chunk:2 * chunk]
    vb = v.astype(BF16)

    row = lax.broadcasted_iota(jnp.int32, (chunk, chunk), 0)
    col = lax.broadcasted_iota(jnp.int32, (chunk, chunk), 1)
    trow = lax.broadcasted_iota(jnp.int32, (chunk, 1), 0)

    attn = jnp.where(row == col, _dot_nt(q.astype(BF16), k.astype(BF16)), 0.0)
    for lv in range(levels):
        ex = jnp.exp(e_all[(2 + lv) * chunk:(3 + lv) * chunk])
        odd = ((trow >> lv) & 1) == 1
        qs = jnp.where(odd, q * ex, 0.0).astype(BF16)
        ks = jnp.where(odd, 0.0, k * ex).astype(BF16)
        pair = ((row >> (lv + 1)) == (col >> (lv + 1))) & (((row >> lv) & 1) == 1) \
            & (((col >> lv) & 1) == 0)
        attn = attn + jnp.where(pair, _dot_nt(qs, ks), 0.0)

    st = st_ref[...]
    o = _dot_nt((q * jnp.exp(b)).astype(BF16), st.astype(BF16)) + _dot(attn.astype(BF16), vb)
    b_last = b[chunk - 1:chunk]
    st_new = jnp.exp(b_last) * st + _dot_tn(vb, (k * jnp.exp(suf)).astype(BF16))
    st_ref[...] = st_new

    @pl.when(c == n_chunks - 1)
    def _():
        sfin_ref[...] = st_new.T

    o = o[0:valid]
    o = o * lax.rsqrt(jnp.mean(o * o, axis=-1, keepdims=True) + RMS_EPS) * ng_ref[...]
    o_ref[...] = (o * _silu(g_ref[...])).astype(o_ref.dtype)


def _hgrn(proj, lb, norm_g, s0, *, batch, seq, out_dtype):
    if seq >= HEAD:
        chunk = valid = HEAD
    else:
        chunk, valid = HGRN_PAD_CHUNK, seq
    n_chunks = seq // valid
    tables, levels = _hgrn_tables(chunk)
    tables = jnp.asarray(tables, BF16)

    def col_spec(part):
        return pl.BlockSpec((valid, HEAD), lambda b, h, c: (b * n_chunks + c, part * N_HEADS + h))

    head_vec = pl.BlockSpec((None, 1, HEAD), lambda b, h, c: (h, 0, 0))
    state_spec = pl.BlockSpec((None, None, HEAD, HEAD), lambda b, h, c: (b, h, 0, 0))
    o, s_fin = pl.pallas_call(
        functools.partial(_hgrn_body, chunk=chunk, valid=valid, levels=levels, n_chunks=n_chunks),
        grid=(batch, N_HEADS, n_chunks),
        in_specs=[col_spec(0), col_spec(1), col_spec(2), col_spec(3), head_vec, head_vec,
                  state_spec, pl.BlockSpec(tables.shape, lambda b, h, c: (0, 0))],
        out_specs=[pl.BlockSpec((valid, HEAD), lambda b, h, c: (b * n_chunks + c, h)), state_spec],
        out_shape=[jax.ShapeDtypeStruct((batch * seq, D_MODEL), out_dtype),
                   jax.ShapeDtypeStruct((batch, N_HEADS, HEAD, HEAD), F32)],
        scratch_shapes=[pltpu.VMEM((HEAD, HEAD), F32)],
        compiler_params=_params(),
        name="hgrn",
    )(proj, proj, proj, proj, lb.reshape(N_HEADS, 1, HEAD), norm_g.reshape(N_HEADS, 1, HEAD),
      s0, tables)
    return o, s_fin


def _select_bias(gate, n_valid_blocks):
    nb = gate.shape[1]
    blk = lax.broadcasted_iota(jnp.int32, gate.shape, 1)
    valid = blk < n_valid_blocks
    beaten = jnp.zeros(gate.shape, jnp.int32)
    for m in range(nb):
        gm = gate[:, m:m + 1]
        wins = (gm > gate) | ((gm == gate) & (m < blk))
        beaten = beaten + jnp.where(wins, 1, 0) * jnp.where(m < n_valid_blocks, 1, 0)
    return jnp.where(valid & (beaten < MOBA_TOPK), 0.0, NEG_INF)


def _moba_seq_body(q_ref, k_ref, v_ref, o_ref, kb_ref, vb_ref, kmean_ref, *, nb):
    i = pl.program_id(2)

    @pl.when(i == 0)
    def _():
        kf = k_ref[...]
        kb_ref[...] = kf.astype(BF16)
        vb_ref[...] = v_ref[...].astype(BF16)
        means = [jnp.mean(kf[n * MOBA_BLOCK:(n + 1) * MOBA_BLOCK], axis=0, keepdims=True)
                 for n in range(nb)]
        kmean_ref[...] = jnp.concatenate(means, axis=0)

    q = q_ref[...]
    qb = q.astype(BF16)
    gate = _dot_nt(q, kmean_ref[...], precision=lax.Precision.HIGHEST)
    bias = _select_bias(gate, i)
    blk = lax.broadcasted_iota(jnp.int32, bias.shape, 1)

    own = pl.multiple_of(i * MOBA_BLOCK, MOBA_BLOCK)
    s = _dot_nt(qb, kb_ref[pl.ds(own, MOBA_BLOCK), :])
    row = lax.broadcasted_iota(jnp.int32, s.shape, 0)
    col = lax.broadcasted_iota(jnp.int32, s.shape, 1)
    s = jnp.where(col <= row, s, NEG_INF)
    m0 = jnp.max(s, axis=-1, keepdims=True)
    p = jnp.exp(s - m0)
    l0 = jnp.sum(p, axis=-1, keepdims=True)
    acc0 = _dot(p.astype(BF16), vb_ref[pl.ds(own, MOBA_BLOCK), :])

    def step(n, carry):
        m_run, l_run, acc = carry
        start = pl.multiple_of(n * MOBA_BLOCK, MOBA_BLOCK)
        bias_n = jnp.sum(jnp.where(blk == n, bias, 0.0), axis=-1, keepdims=True)
        s = _dot_nt(qb, kb_ref[pl.ds(start, MOBA_BLOCK), :]) + bias_n
        m_new = jnp.maximum(m_run, jnp.max(s, axis=-1, keepdims=True))
        a = jnp.exp(m_run - m_new)
        p = jnp.exp(s - m_new)
        l_new = a * l_run + jnp.sum(p, axis=-1, keepdims=True)
        acc = a * acc + _dot(p.astype(BF16), vb_ref[pl.ds(start, MOBA_BLOCK), :])
        return m_new, l_new, acc

    _, l_fin, acc = lax.fori_loop(0, i, step, (m0, l0, acc0))
    o_ref[...] = (acc / l_fin).astype(o_ref.dtype)


def _moba_seq(q, kv, *, batch, seq):
    nb = seq // MOBA_BLOCK
    return pl.pallas_call(
        functools.partial(_moba_seq_body, nb=nb),
        grid=(batch, N_HEADS, nb),
        in_specs=[pl.BlockSpec((MOBA_BLOCK, HEAD), lambda b, h, i: (b * nb + i, h)),
                  pl.BlockSpec((seq, HEAD), lambda b, h, i: (b, h)),
                  pl.BlockSpec((seq, HEAD), lambda b, h, i: (b, N_HEADS + h))],
        out_specs=pl.BlockSpec((MOBA_BLOCK, HEAD), lambda b, h, i: (b * nb + i, h)),
        out_shape=jax.ShapeDtypeStruct((batch * seq, D_MODEL), BF16),
        scratch_shapes=[pltpu.VMEM((seq, HEAD), BF16), pltpu.VMEM((seq, HEAD), BF16),
                        pltpu.VMEM((nb, HEAD), F32)],
        compiler_params=_params(),
        name="moba_prompt",
    )(q, kv, kv)


def _moba_paged_body(pt_ref, q_ref, kn_ref, vn_ref, k0_ref, k1_ref, v0_ref, v1_ref, o_ref,
                     qw_ref, qwb_ref, kmean_ref, s_ref, p_ref, acc_ref, *, nb, tq):
    del pt_ref
    j = pl.program_id(1)
    rows = N_HEADS * tq
    pad = 16

    @pl.when(j == 0)
    def _():
        qt = jnp.concatenate([q_ref[...]] * N_HEADS, axis=0)
        r = lax.broadcasted_iota(jnp.int32, qt.shape, 0)
        cc = lax.broadcasted_iota(jnp.int32, qt.shape, 1)
        qw = jnp.where((r // tq) == (cc // HEAD), qt, 0.0)
        qw_ref[...] = qw
        qwb_ref[...] = qw.astype(BF16)

    @pl.when(j < nb)
    def _():
        kblk = jnp.concatenate([k0_ref[...], k1_ref[...]], axis=0)
        kmean_ref[pl.ds(j, 1), :] = jnp.mean(kblk, axis=0, keepdims=True)
        s_ref[j] = _dot_nt(qwb_ref[...], kblk.astype(BF16))

    @pl.when(j == nb)
    def _():
        gate = _dot_nt(qw_ref[...], kmean_ref[...], precision=lax.Precision.HIGHEST)
        bias = _select_bias(gate, nb)
        zeros = jnp.zeros((pad - tq, D_MODEL), F32)
        kn = jnp.concatenate([kn_ref[...], zeros], axis=0).astype(BF16)
        vn = jnp.concatenate([vn_ref[...], zeros], axis=0).astype(BF16)
        s_own = _dot_nt(qwb_ref[...], kn)
        r = lax.broadcasted_iota(jnp.int32, s_own.shape, 0)
        cc = lax.broadcasted_iota(jnp.int32, s_own.shape, 1)
        s_own = jnp.where(cc <= (r % tq), s_own, NEG_INF)
        m = jnp.max(s_own, axis=-1, keepdims=True)
        for n in range(nb):
            m = jnp.maximum(m, jnp.max(s_ref[n] + bias[:, n:n + 1], axis=-1, keepdims=True))
        e_own = jnp.exp(s_own - m)
        l = jnp.sum(e_own, axis=-1, keepdims=True)
        for n in range(nb):
            l = l + jnp.sum(jnp.exp(s_ref[n] + bias[:, n:n + 1] - m), axis=-1, keepdims=True)
        inv = 1.0 / l
        for n in range(nb):
            p_ref[n] = (jnp.exp(s_ref[n] + bias[:, n:n + 1] - m) * inv).astype(BF16)
        acc_ref[...] = _dot((e_own * inv).astype(BF16), vn)

    @pl.when(j >= nb)
    def _():
        vblk = jnp.concatenate([v0_ref[...], v1_ref[...]], axis=0).astype(BF16)
        acc_ref[...] += _dot(p_ref[j - nb], vblk)

    @pl.when(j == 2 * nb - 1)
    def _():
        for h in range(N_HEADS):
            o_ref[:, h * HEAD:(h + 1) * HEAD] = acc_ref[h * tq:(h + 1) * tq, h * HEAD:(h + 1) * HEAD]


def _moba_paged(q, kv_new, cache_k, cache_v, page_table, *, batch, tq):
    n_pages = page_table.shape[1]
    nb = n_pages // PAGES_PER_BLOCK
    n_phys = cache_k.shape[0]
    ck = cache_k.reshape(n_phys, PAGE_SIZE, D_MODEL)
    cv = cache_v.reshape(n_phys, PAGE_SIZE, D_MODEL)
    rows = N_HEADS * tq

    def k_page(p):
        return pl.BlockSpec(
            (None, PAGE_SIZE, D_MODEL),
            lambda b, j, pt: (pt[b, PAGES_PER_BLOCK * jnp.minimum(j, nb - 1) + p], 0, 0))

    def v_page(p):
        return pl.BlockSpec(
            (None, PAGE_SIZE, D_MODEL),
            lambda b, j, pt: (pt[b, PAGES_PER_BLOCK * jnp.maximum(j - nb, 0) + p], 0, 0))

    grid_spec = pltpu.PrefetchScalarGridSpec(
        num_scalar_prefetch=1,
        grid=(batch, 2 * nb),
        in_specs=[pl.BlockSpec((tq, D_MODEL), lambda b, j, pt: (b, 0)),
                  pl.BlockSpec((tq, D_MODEL), lambda b, j, pt: (b, 0)),
                  pl.BlockSpec((tq, D_MODEL), lambda b, j, pt: (b, 1)),
                  k_page(0), k_page(1), v_page(0), v_page(1)],
        out_specs=pl.BlockSpec((tq, D_MODEL), lambda b, j, pt: (b, 0)),
        scratch_shapes=[pltpu.VMEM((rows, D_MODEL), F32), pltpu.VMEM((rows, D_MODEL), BF16),
                        pltpu.VMEM((nb, D_MODEL), F32),
                        pltpu.VMEM((nb, rows, MOBA_BLOCK), F32),
                        pltpu.VMEM((nb, rows, MOBA_BLOCK), BF16),
                        pltpu.VMEM((rows, D_MODEL), F32)],
    )
    return pl.pallas_call(
        functools.partial(_moba_paged_body, nb=nb, tq=tq),
        grid_spec=grid_spec,
        out_shape=jax.ShapeDtypeStruct((batch * tq, D_MODEL), F32),
        compiler_params=_params(),
        name="moba_paged",
    )(page_table, q, kv_new, kv_new, ck, ck, cv, cv)


def _tile_sizes(m):
    if m >= 1024:
        return dict(tm_wide=1024, tm_ln=512)
    return dict(tm_wide=m, tm_ln=m)


def _trunk(x, mod, mod_kv, s0, attend, weights, *, batch, seq):
    (a_w_in, lb_all, a_norm_g, a_w_out, b_w_q, b_w_o, w_kv, ln_g, ln_b, ffn_w_in, ffn_w_out) = weights
    m = batch * seq
    x = x.reshape(m, D_MODEL)
    ts = _tile_sizes(m)
    tm_wide, tm_ln = ts["tm_wide"], ts["tm_ln"]
    per_row = seq < tm_ln
    rows_per_group = m if per_row else seq

    def vecs(mvec, n):
        out = []
        for part in jnp.split(mvec, n, axis=-1):
            if per_row:
                out.append(jnp.repeat(part, seq, axis=0).reshape(1, m, D_MODEL))
            else:
                out.append(part.reshape(batch, 1, D_MODEL))
        return out

    sh1, sc1, g1, sh2, sc2, g2 = vecs(mod[0], 6)
    sh1b, sc1b, g1b, sh2b, sc2b, g2b = vecs(mod[1], 6)
    shift_kv, scale_kv = vecs(mod_kv, 2)
    common = dict(tm=tm_ln, tk=512, rows_per_group=rows_per_group)

    h = _modulate(x, sc1, sh1, tm=tm_ln, rows_per_group=rows_per_group)
    proj = _matmul(h, a_w_in[0], tm=tm_wide, tn=512, name="hgrn_in_proj")
    o, s_fin = _hgrn(proj, lb_all[0], a_norm_g[0], s0[0], batch=batch, seq=seq,
                     out_dtype=BF16 if seq >= HEAD else F32)
    x, (h,) = _matmul_ln(o, a_w_out[0], x, g1, ln_g[0, 0], ln_b[0, 0], [(sc2, sh2)],
                         name="hgrn_out_proj", **common)
    hid = _swiglu_in(h, ffn_w_in[0], tm=tm_wide, tn=512)
    x, (h, z) = _matmul_ln(hid, ffn_w_out[0], x, g2, ln_g[0, 1], ln_b[0, 1],
                           [(sc1b, sh1b), (scale_kv, shift_kv)], name="ffn0_out_proj", **common)

    kv = _matmul(z, w_kv, tm=tm_wide, tn=512, name="kv_proj")
    q = _matmul(h, b_w_q[0], scale=HEAD ** -0.5, tm=tm_wide, tn=512, name="q_proj")
    o = attend(q, kv)
    x, (h,) = _matmul_ln(o, b_w_o[0], x, g1b, ln_g[1, 0], ln_b[1, 0], [(sc2b, sh2b)],
                         name="moba_out_proj", **common)
    hid = _swiglu_in(h, ffn_w_in[1], tm=tm_wide, tn=512)
    y, _ = _matmul_ln(hid, ffn_w_out[1], x, g2b, ln_g[1, 1], ln_b[1, 1], [],
                      name="ffn1_out_proj", **common)

    k_sh = kv[:, :D_MODEL].reshape(batch, seq, N_HEADS, HEAD)
    v_sh = kv[:, D_MODEL:].reshape(batch, seq, N_HEADS, HEAD)
    return y.reshape(batch, seq, D_MODEL), s_fin[None], k_sh, v_sh


def kernel(x_prompt, x_sample, state_hgrn, cache_k, cache_v, page_table, c_prompt, c_sample,
           a_w_in, a_lb_logits, a_norm_g, a_w_out, b_w_q, b_w_o, w_ada_kv, b_ada_kv, w_kv,
           w_ada, b_ada, ln_g, ln_b, ffn_w_in, ffn_w_out):
    assert a_w_in.shape[0] == 1 and b_w_q.shape[0] == 1 and w_ada.shape[0] == DEPTH
    bp, tp, _ = x_prompt.shape
    bs, tsamp, _ = x_sample.shape

    lb_all = _lower_bounds(a_lb_logits.astype(F32))

    c_all = jnp.concatenate([c_prompt, c_sample], axis=0)
    n_c = c_all.shape[0]
    c_rows = -(-n_c // 16) * 16
    c_all = jnp.pad(c_all, ((0, c_rows - n_c), (0, 0)))
    ada = functools.partial(_matmul, silu_in=True, tm=c_rows, tn=1024)
    mod = [ada(c_all, w_ada[l], bias=b_ada[l], name="ada_layer") for l in range(DEPTH)]
    mod_kv = ada(c_all, w_ada_kv, bias=b_ada_kv, name="ada_kv")

    weights = (a_w_in, lb_all, a_norm_g, a_w_out, b_w_q, b_w_o, w_kv, ln_g, ln_b, ffn_w_in, ffn_w_out)

    s0_prompt = jnp.zeros((a_w_in.shape[0], bp, N_HEADS, HEAD, HEAD), state_hgrn.dtype)
    y_p, s_p, k_p, v_p = _trunk(
        x_prompt, [mm[:bp] for mm in mod], mod_kv[:bp], s0_prompt,
        functools.partial(_moba_seq, batch=bp, seq=tp), weights, batch=bp, seq=tp)
    y_s, s_s, k_s, v_s = _trunk(
        x_sample, [mm[bp:bp + bs] for mm in mod], mod_kv[bp:bp + bs], state_hgrn,
        lambda q, kv: _moba_paged(q, kv, cache_k, cache_v, page_table, batch=bs, tq=tsamp),
        weights, batch=bs, seq=tsamp)
    return (y_p, y_s, s_p, s_s, k_p, v_p, k_s, v_s)
```

```python
import functools

import numpy as np
import jax
import jax.numpy as jnp
from jax import lax
from jax.experimental import pallas as pl
from jax.experimental.pallas import tpu as pltpu

F32 = jnp.float32
BF16 = jnp.bfloat16

D_MODEL = 2048
DEPTH = 2
HEAD = 128
N_HEADS = D_MODEL // HEAD
HGRN_PAD_CHUNK = 16
HGRN_HEADS_PER_STEP = 4
MOBA_BLOCK = 256
MOBA_TOPK = 3
PAGE_SIZE = 128
PAGES_PER_BLOCK = MOBA_BLOCK // PAGE_SIZE
BF16_ROWS = 16
LN_EPS = 1e-5
RMS_EPS = 1e-6
ALPHA = (2.0 * DEPTH) ** 0.25
VMEM_LIMIT_BYTES = 56 * 1024 * 1024

NEG_INF = float("-inf")


def _params():
    return pltpu.CompilerParams(vmem_limit_bytes=VMEM_LIMIT_BYTES)


def _dot(a, b, precision=None):
    return lax.dot_general(a, b, (((1,), (0,)), ((), ())), preferred_element_type=F32,
                           precision=precision)


def _dot_nt(a, b, precision=None):
    return lax.dot_general(a, b, (((1,), (1,)), ((), ())), preferred_element_type=F32,
                           precision=precision)


def _dot_tn(a, b):
    return lax.dot_general(a, b, (((0,), (0,)), ((), ())), preferred_element_type=F32)


def _silu(x):
    return x * jax.nn.sigmoid(x)


def _lb_body(logit_ref, out_ref):
    z = logit_ref[...]
    n = z.shape[0]
    m = z[0:1]
    for r in range(1, n):
        m = jnp.maximum(m, z[r:r + 1])
    e = jnp.exp(z - m)
    tot = e[0:1]
    for r in range(1, n):
        tot = tot + e[r:r + 1]
    p = e / tot
    run = p[0:1]
    out_ref[0:1, :] = run
    for r in range(1, n):
        run = run + p[r:r + 1]
        out_ref[r:r + 1, :] = run


def _lower_bounds(logits):
    return pl.pallas_call(
        _lb_body, out_shape=jax.ShapeDtypeStruct(logits.shape, F32), name="lower_bounds")(logits)


def _mm_body(*refs, silu_in, scale, has_bias):
    if has_bias:
        x_ref, w_ref, b_ref, o_ref, wb_ref = refs
    else:
        x_ref, w_ref, o_ref, wb_ref = refs

    @pl.when(pl.program_id(1) == 0)
    def _():
        wb_ref[...] = w_ref[...].astype(BF16)

    x = x_ref[...]
    if silu_in:
        x = _silu(x)
    acc = _dot(x.astype(BF16), wb_ref[...])
    if has_bias:
        acc = acc + b_ref[...]
    if scale is not None:
        acc = acc * scale
    o_ref[...] = acc.astype(o_ref.dtype)


def _matmul(x, w, *, bias=None, silu_in=False, scale=None, tm, tn, name):
    m, k = x.shape
    n = w.shape[1]
    in_specs = [pl.BlockSpec((tm, k), lambda j, i: (i, 0)),
                pl.BlockSpec((k, tn), lambda j, i: (0, j))]
    args = [x, w]
    if bias is not None:
        in_specs.append(pl.BlockSpec((1, tn), lambda j, i: (0, j)))
        args.append(bias.reshape(1, n))
    return pl.pallas_call(
        functools.partial(_mm_body, silu_in=silu_in, scale=scale, has_bias=bias is not None),
        grid=(n // tn, m // tm),
        in_specs=in_specs,
        out_specs=pl.BlockSpec((tm, tn), lambda j, i: (i, j)),
        out_shape=jax.ShapeDtypeStruct((m, n), F32),
        scratch_shapes=[pltpu.VMEM((k, tn), BF16)],
        compiler_params=_params(),
        name=name,
    )(*args)


def _pair_body(*refs, swiglu):
    if swiglu:
        x_ref, wa_ref, wu_ref, o_ref, wab_ref, wub_ref = refs
    else:
        x_ref, wa_ref, wu_ref, oa_ref, ou_ref, wab_ref, wub_ref = refs

    @pl.when(pl.program_id(1) == 0)
    def _():
        wab_ref[...] = wa_ref[...].astype(BF16)
        wub_ref[...] = wu_ref[...].astype(BF16)

    x = x_ref[...]
    a = _dot(x, wab_ref[...])
    u = _dot(x, wub_ref[...])
    if swiglu:
        o_ref[...] = (_silu(a) * u).astype(o_ref.dtype)
    else:
        oa_ref[...] = a
        ou_ref[...] = u


def _matmul_pair(x, w, *, swiglu, tm, tn, name):
    m, k = x.shape
    f = w.shape[1] // 2
    nj = f // tn
    out_spec = pl.BlockSpec((tm, tn), lambda j, i: (i, j))
    if swiglu:
        out_specs, out_shape = out_spec, jax.ShapeDtypeStruct((m, f), BF16)
    else:
        out_specs, out_shape = [out_spec] * 2, [jax.ShapeDtypeStruct((m, f), F32)] * 2
    return pl.pallas_call(
        functools.partial(_pair_body, swiglu=swiglu),
        grid=(nj, m // tm),
        in_specs=[pl.BlockSpec((tm, k), lambda j, i: (i, 0)),
                  pl.BlockSpec((k, tn), lambda j, i: (0, j)),
                  pl.BlockSpec((k, tn), lambda j, i: (0, j + nj))],
        out_specs=out_specs,
        out_shape=out_shape,
        scratch_shapes=[pltpu.VMEM((k, tn), BF16), pltpu.VMEM((k, tn), BF16)],
        compiler_params=_params(),
        name=name,
    )(x, w, w)


def _modulate_body(x_ref, sc_ref, sh_ref, o_ref):
    o_ref[...] = (x_ref[...] * (1.0 + sc_ref[...]) + sh_ref[...]).astype(o_ref.dtype)


def _mod_spec(vec, tm, rows_per_group):
    r = vec.shape[1]
    tiles = rows_per_group // tm
    return pl.BlockSpec((None, r, D_MODEL), lambda i, *_: (i // tiles, 0, 0))


def _modulate(x, sc, sh, *, tm, rows_per_group):
    m = x.shape[0]
    return pl.pallas_call(
        _modulate_body,
        grid=(m // tm,),
        in_specs=[pl.BlockSpec((tm, D_MODEL), lambda i: (i, 0)),
                  _mod_spec(sc, tm, rows_per_group), _mod_spec(sh, tm, rows_per_group)],
        out_specs=pl.BlockSpec((tm, D_MODEL), lambda i: (i, 0)),
        out_shape=jax.ShapeDtypeStruct((m, D_MODEL), BF16),
        compiler_params=_params(),
        name="modulate",
    )(x, sc, sh)


def _mm_ln_body(*refs, n_mod, nk):
    a_ref, w_ref, x_ref, gate_ref, lng_ref, lnb_ref = refs[:6]
    mod_refs = refs[6:6 + 2 * n_mod]
    y_ref = refs[6 + 2 * n_mod]
    h_refs = refs[7 + 2 * n_mod:7 + 3 * n_mod]
    acc_ref = refs[7 + 3 * n_mod]
    kk = pl.program_id(1)

    @pl.when(kk == 0)
    def _():
        acc_ref[...] = jnp.zeros_like(acc_ref)

    acc_ref[...] += _dot(a_ref[...].astype(BF16), w_ref[...].astype(BF16))

    @pl.when(kk == nk - 1)
    def _():
        r = ALPHA * x_ref[...] + (1.0 + gate_ref[...]) * acc_ref[...]
        mu = jnp.mean(r, axis=-1, keepdims=True)
        d = r - mu
        var = jnp.mean(d * d, axis=-1, keepdims=True)
        y = d * lax.rsqrt(var + LN_EPS) * lng_ref[...] + lnb_ref[...]
        y_ref[...] = y
        for j in range(n_mod):
            sc_ref, sh_ref = mod_refs[2 * j], mod_refs[2 * j + 1]
            h_refs[j][...] = (y * (1.0 + sc_ref[...]) + sh_ref[...]).astype(BF16)


def _matmul_ln(a, w, x, gate, ln_g, ln_b, mods, *, tm, tk, rows_per_group, name):
    m, k = a.shape
    nk = k // tk
    n_mod = len(mods)
    row_spec = pl.BlockSpec((tm, D_MODEL), lambda i, kk: (i, 0))
    vec_spec = pl.BlockSpec((1, D_MODEL), lambda i, kk: (0, 0))
    in_specs = [pl.BlockSpec((tm, tk), lambda i, kk: (i, kk)),
                pl.BlockSpec((tk, D_MODEL), lambda i, kk: (kk, 0)),
                row_spec, _mod_spec(gate, tm, rows_per_group), vec_spec, vec_spec]
    args = [a, w, x, gate, ln_g.reshape(1, D_MODEL), ln_b.reshape(1, D_MODEL)]
    for sc, sh in mods:
        in_specs += [_mod_spec(sc, tm, rows_per_group), _mod_spec(sh, tm, rows_per_group)]
        args += [sc, sh]
    out = pl.pallas_call(
        functools.partial(_mm_ln_body, n_mod=n_mod, nk=nk),
        grid=(m // tm, nk),
        in_specs=in_specs,
        out_specs=[row_spec] * (1 + n_mod),
        out_shape=[jax.ShapeDtypeStruct((m, D_MODEL), F32)]
        + [jax.ShapeDtypeStruct((m, D_MODEL), BF16)] * n_mod,
        scratch_shapes=[pltpu.VMEM((tm, D_MODEL), F32)],
        compiler_params=_params(),
        name=name,
    )(*args)
    return out[0], list(out[1:])


def _hgrn_tables(chunk):
    levels = chunk.bit_length() - 1
    t = np.arange(chunk)[:, None]
    r = np.arange(chunk)[None, :]
    mats = [(r <= t), (r > t)]
    for lv in range(levels):
        size = 1 << lv
        odd = ((t >> lv) & 1) == 1
        start = t & ~(size - 1)
        end = t | (size - 1)
        mats.append(np.where(odd, (r >= start) & (r <= t), (r > t) & (r <= end)))
    tab = np.concatenate(mats, axis=0).astype(np.float32)
    return np.concatenate([tab] * 3, axis=1), levels


def _hgrn_head(q, fl, v, g, lb, ng, st, tab3, *, chunk, valid, levels):
    f = lb + (1.0 - lb) * jax.nn.sigmoid(fl)
    lf = jnp.log(f)
    k = 1.0 - f
    if valid < chunk:
        zeros = jnp.zeros((chunk - valid, HEAD), F32)
        lf, k, q, v = (jnp.concatenate([a, zeros], axis=0) for a in (lf, k, q, v))

    hi = lf.astype(BF16)
    r1 = lf - hi.astype(F32)
    mid = r1.astype(BF16)
    lo = (r1 - mid.astype(F32)).astype(BF16)
    e_all = _dot(tab3, jnp.concatenate([hi, mid, lo], axis=0))

    b = e_all[0:chunk]
    suf = e_all[chunk:2 * chunk]
    vb = v.astype(BF16)

    row = lax.broadcasted_iota(jnp.int32, (chunk, chunk), 0)
    col = lax.broadcasted_iota(jnp.int32, (chunk, chunk), 1)
    trow = lax.broadcasted_iota(jnp.int32, (chunk, 1), 0)

    attn = jnp.where(row == col, _dot_nt(q.astype(BF16), k.astype(BF16)), 0.0)
    for lv in range(levels):
        ex = jnp.exp(e_all[(2 + lv) * chunk:(3 + lv) * chunk])
        odd = ((trow >> lv) & 1) == 1
        qs = jnp.where(odd, q * ex, 0.0).astype(BF16)
        ks = jnp.where(odd, 0.0, k * ex).astype(BF16)
        pair = ((row >> (lv + 1)) == (col >> (lv + 1))) & (((row >> lv) & 1) == 1) \
            & (((col >> lv) & 1) == 0)
        attn = attn + jnp.where(pair, _dot_nt(qs, ks), 0.0)

    o = _dot_nt((q * jnp.exp(b)).astype(BF16), st.astype(BF16)) + _dot(attn.astype(BF16), vb)
    b_last = b[chunk - 1:chunk]
    st_new = jnp.exp(b_last) * st + _dot_tn(vb, (k * jnp.exp(suf)).astype(BF16))

    o = o[0:valid]
    o = o * lax.rsqrt(jnp.mean(o * o, axis=-1, keepdims=True) + RMS_EPS) * ng
    return o * _silu(g), st_new


def _hgrn_body(q_ref, f_ref, i_ref, g_ref, lb_ref, ng_ref, s0_ref, tab_ref,
               o_ref, sfin_ref, st_ref, *, chunk, valid, levels, n_chunks, heads):
    c = pl.program_id(2)
    tab3 = tab_ref[...]
    for i in range(heads):
        cols = slice(i * HEAD, (i + 1) * HEAD)

        @pl.when(c == 0)
        def _():
            st_ref[i] = s0_ref[i].T

        o, st_new = _hgrn_head(q_ref[:, cols], f_ref[:, cols], i_ref[:, cols], g_ref[:, cols],
                               lb_ref[:, cols], ng_ref[:, cols], st_ref[i], tab3,
                               chunk=chunk, valid=valid, levels=levels)
        st_ref[i] = st_new
        o_ref[:, cols] = o.astype(o_ref.dtype)

        @pl.when(c == n_chunks - 1)
        def _():
            sfin_ref[i] = st_new.T


def _hgrn(proj, lb, norm_g, s0, *, batch, seq, out_dtype):
    if seq >= HEAD:
        chunk = valid = HEAD
        heads = HGRN_HEADS_PER_STEP
    else:
        chunk, valid = HGRN_PAD_CHUNK, seq
        heads = N_HEADS
    n_chunks = seq // valid
    groups = N_HEADS // heads
    tables, levels = _hgrn_tables(chunk)
    tables = jnp.asarray(tables, BF16)
    width = heads * HEAD

    def col_spec(part):
        return pl.BlockSpec((valid, width), lambda b, h, c: (b * n_chunks + c, part * groups + h))

    head_vec = pl.BlockSpec((1, width), lambda b, h, c: (0, h))
    state_spec = pl.BlockSpec((None, heads, HEAD, HEAD), lambda b, h, c: (b, h, 0, 0))
    o, s_fin = pl.pallas_call(
        functools.partial(_hgrn_body, chunk=chunk, valid=valid, levels=levels,
                          n_chunks=n_chunks, heads=heads),
        grid=(batch, groups, n_chunks),
        in_specs=[col_spec(0), col_spec(1), col_spec(2), col_spec(3), head_vec, head_vec,
                  state_spec, pl.BlockSpec(tables.shape, lambda b, h, c: (0, 0))],
        out_specs=[pl.BlockSpec((valid, width), lambda b, h, c: (b * n_chunks + c, h)), state_spec],
        out_shape=[jax.ShapeDtypeStruct((batch * seq, D_MODEL), out_dtype),
                   jax.ShapeDtypeStruct((batch, N_HEADS, HEAD, HEAD), F32)],
        scratch_shapes=[pltpu.VMEM((heads, HEAD, HEAD), F32)],
        compiler_params=_params(),
        name="hgrn",
    )(proj, proj, proj, proj, lb.reshape(1, D_MODEL), norm_g.reshape(1, D_MODEL), s0, tables)
    return o, s_fin


def _select_bias(gate, n_valid_blocks, axis):
    nb = gate.shape[axis]
    blk = lax.broadcasted_iota(jnp.int32, gate.shape, axis)
    valid = blk < n_valid_blocks
    beaten = jnp.zeros(gate.shape, jnp.int32)
    for m in range(nb):
        gm = gate[m:m + 1, :] if axis == 0 else gate[:, m:m + 1]
        wins = (gm > gate) | ((gm == gate) & (m < blk))
        beaten = beaten + jnp.where(wins, 1, 0) * jnp.where(m < n_valid_blocks, 1, 0)
    return jnp.where(valid & (beaten < MOBA_TOPK), 0.0, NEG_INF)


def _moba_seq_body(q_ref, k_ref, v_ref, o_ref, kb_ref, vt_ref, kmean_ref, bias_ref, *, nb):
    i = pl.program_id(2)

    @pl.when(i == 0)
    def _():
        kf = k_ref[...]
        kb_ref[...] = kf.astype(BF16)
        means = []
        for n in range(nb):
            rows = slice(n * MOBA_BLOCK, (n + 1) * MOBA_BLOCK)
            vt_ref[n] = v_ref[rows, :].T.astype(BF16)
            means.append(jnp.mean(kf[rows], axis=0, keepdims=True))
        kmean_ref[...] = jnp.concatenate(means, axis=0)

    qt = q_ref[...].T
    qtb = qt.astype(BF16)
    gate = _dot(kmean_ref[...], qt, precision=lax.Precision.HIGHEST)
    bias_ref[...] = _select_bias(gate, i, axis=0)

    own = pl.multiple_of(i * MOBA_BLOCK, MOBA_BLOCK)
    s = _dot(kb_ref[pl.ds(own, MOBA_BLOCK), :], qtb)
    key = lax.broadcasted_iota(jnp.int32, s.shape, 0)
    qry = lax.broadcasted_iota(jnp.int32, s.shape, 1)
    s = jnp.where(key <= qry, s, NEG_INF)
    m0 = jnp.max(s, axis=0, keepdims=True)
    p = jnp.exp(s - m0)
    l0 = jnp.sum(p, axis=0, keepdims=True)
    acc0 = _dot(vt_ref[i], p.astype(BF16))

    def step(n, carry):
        m_run, l_run, acc = carry
        start = pl.multiple_of(n * MOBA_BLOCK, MOBA_BLOCK)
        s = _dot(kb_ref[pl.ds(start, MOBA_BLOCK), :], qtb) + bias_ref[pl.ds(n, 1), :]
        m_new = jnp.maximum(m_run, jnp.max(s, axis=0, keepdims=True))
        a = jnp.exp(m_run - m_new)
        p = jnp.exp(s - m_new)
        l_new = a * l_run + jnp.sum(p, axis=0, keepdims=True)
        acc = a * acc + _dot(vt_ref[n], p.astype(BF16))
        return m_new, l_new, acc

    _, l_fin, acc = lax.fori_loop(0, i, step, (m0, l0, acc0))
    o_ref[...] = (acc / l_fin).T.astype(o_ref.dtype)


def _moba_seq(q, k, v, *, batch, seq):
    nb = seq // MOBA_BLOCK
    return pl.pallas_call(
        functools.partial(_moba_seq_body, nb=nb),
        grid=(batch, N_HEADS, nb),
        in_specs=[pl.BlockSpec((MOBA_BLOCK, HEAD), lambda b, h, i: (b * nb + i, h)),
                  pl.BlockSpec((seq, HEAD), lambda b, h, i: (b, h)),
                  pl.BlockSpec((seq, HEAD), lambda b, h, i: (b, h))],
        out_specs=pl.BlockSpec((MOBA_BLOCK, HEAD), lambda b, h, i: (b * nb + i, h)),
        out_shape=jax.ShapeDtypeStruct((batch * seq, D_MODEL), BF16),
        scratch_shapes=[pltpu.VMEM((seq, HEAD), BF16), pltpu.VMEM((nb, HEAD, MOBA_BLOCK), BF16),
                        pltpu.VMEM((nb, HEAD), F32), pltpu.VMEM((nb, MOBA_BLOCK), F32)],
        compiler_params=_params(),
        name="moba_prompt",
    )(q, k, v)


def _moba_paged_body(pt_ref, q_ref, kn_ref, vn_ref, k0_ref, k1_ref, v0_ref, v1_ref, o_ref,
                     qw_ref, qh_ref, kmean_ref, s_ref, acc_ref, *, nb, tq):
    del pt_ref
    j = pl.program_id(1)
    pad_rows = jnp.zeros((BF16_ROWS - tq, MOBA_BLOCK), F32)

    def head_block(p0_ref, p1_ref, h):
        return jnp.concatenate([p0_ref[:, h, :], p1_ref[:, h, :]], axis=0)

    @pl.when(j == 0)
    def _():
        q = q_ref[...]
        qt = jnp.concatenate([q] * N_HEADS, axis=0)
        r = lax.broadcasted_iota(jnp.int32, qt.shape, 0)
        cc = lax.broadcasted_iota(jnp.int32, qt.shape, 1)
        qw_ref[...] = jnp.where((r // tq) == (cc // HEAD), qt, 0.0)
        zeros = jnp.zeros((BF16_ROWS - tq, HEAD), F32)
        for h in range(N_HEADS):
            qh = jnp.concatenate([q[:, h * HEAD:(h + 1) * HEAD], zeros], axis=0)
            qh_ref[h] = qh.astype(BF16)

    @pl.when(j < nb)
    def _():
        means = []
        for h in range(N_HEADS):
            kh = head_block(k0_ref, k1_ref, h)
            means.append(jnp.mean(kh, axis=0, keepdims=True))
            s = _dot_nt(qh_ref[h], kh.astype(BF16))
            s_ref[j, h * tq:(h + 1) * tq, :] = s[0:tq]
        kmean_ref[pl.ds(j, 1), :] = jnp.concatenate(means, axis=1)

    @pl.when(j == nb)
    def _():
        qw = qw_ref[...]
        gate = _dot_nt(qw, kmean_ref[...], precision=lax.Precision.HIGHEST)
        bias = _select_bias(gate, nb, axis=1)
        zeros = jnp.zeros((BF16_ROWS - tq, D_MODEL), F32)
        kn = jnp.concatenate([kn_ref[...], zeros], axis=0).astype(BF16)
        vn = jnp.concatenate([vn_ref[...], zeros], axis=0).astype(BF16)
        s_own = _dot_nt(qw.astype(BF16), kn)
        r = lax.broadcasted_iota(jnp.int32, s_own.shape, 0)
        cc = lax.broadcasted_iota(jnp.int32, s_own.shape, 1)
        s_own = jnp.where(cc <= (r % tq), s_own, NEG_INF)
        m = jnp.max(s_own, axis=-1, keepdims=True)
        for n in range(nb):
            m = jnp.maximum(m, jnp.max(s_ref[n] + bias[:, n:n + 1], axis=-1, keepdims=True))
        e_own = jnp.exp(s_own - m)
        l = jnp.sum(e_own, axis=-1, keepdims=True)
        for n in range(nb):
            l = l + jnp.sum(jnp.exp(s_ref[n] + bias[:, n:n + 1] - m), axis=-1, keepdims=True)
        inv = 1.0 / l
        for n in range(nb):
            s_ref[n] = jnp.exp(s_ref[n] + bias[:, n:n + 1] - m) * inv
        own = _dot((e_own * inv).astype(BF16), vn)
        for h in range(N_HEADS):
            cols = slice(h * HEAD, (h + 1) * HEAD)
            acc_ref[:, cols] = own[h * tq:(h + 1) * tq, cols]

    @pl.when(j >= nb)
    def _():
        n = j - nb
        for h in range(N_HEADS):
            cols = slice(h * HEAD, (h + 1) * HEAD)
            vh = head_block(v0_ref, v1_ref, h).astype(BF16)
            ph = jnp.concatenate([s_ref[n, h * tq:(h + 1) * tq, :], pad_rows], axis=0)
            acc_ref[:, cols] += _dot(ph.astype(BF16), vh)[0:tq]

    @pl.when(j == 2 * nb - 1)
    def _():
        o_ref[...] = acc_ref[...]


def _moba_paged(q, k_new, v_new, cache_k, cache_v, page_table, *, batch, tq):
    n_pages = page_table.shape[1]
    nb = n_pages // PAGES_PER_BLOCK
    rows = N_HEADS * tq
    page_block = (None, PAGE_SIZE, N_HEADS, HEAD)

    def k_page(p):
        return pl.BlockSpec(
            page_block,
            lambda b, j, pt: (pt[b, PAGES_PER_BLOCK * jnp.minimum(j, nb - 1) + p], 0, 0, 0))

    def v_page(p):
        return pl.BlockSpec(
            page_block,
            lambda b, j, pt: (pt[b, PAGES_PER_BLOCK * jnp.maximum(j - nb, 0) + p], 0, 0, 0))

    new_spec = pl.BlockSpec((tq, D_MODEL), lambda b, j, pt: (b, 0))
    grid_spec = pltpu.PrefetchScalarGridSpec(
        num_scalar_prefetch=1,
        grid=(batch, 2 * nb),
        in_specs=[new_spec, new_spec, new_spec, k_page(0), k_page(1), v_page(0), v_page(1)],
        out_specs=new_spec,
        scratch_shapes=[pltpu.VMEM((rows, D_MODEL), F32),
                        pltpu.VMEM((N_HEADS, BF16_ROWS, HEAD), BF16),
                        pltpu.VMEM((nb, D_MODEL), F32),
                        pltpu.VMEM((nb, rows, MOBA_BLOCK), F32),
                        pltpu.VMEM((tq, D_MODEL), F32)],
    )
    return pl.pallas_call(
        functools.partial(_moba_paged_body, nb=nb, tq=tq),
        grid_spec=grid_spec,
        out_shape=jax.ShapeDtypeStruct((batch * tq, D_MODEL), F32),
        compiler_params=_params(),
        name="moba_paged",
    )(page_table, q, k_new, v_new, cache_k, cache_k, cache_v, cache_v)


def _tile_sizes(m):
    if m >= 1024:
        return dict(tm_wide=1024, tm_ln=512)
    return dict(tm_wide=m, tm_ln=m)


def _trunk(x, mod, mod_kv, s0, attend, weights, *, batch, seq):
    (a_w_in, lb_all, a_norm_g, a_w_out, b_w_q, b_w_o, w_kv, ln_g, ln_b, ffn_w_in, ffn_w_out) = weights
    m = batch * seq
    x = x.reshape(m, D_MODEL)
    ts = _tile_sizes(m)
    tm_wide, tm_ln = ts["tm_wide"], ts["tm_ln"]
    per_row = seq < tm_ln
    rows_per_group = m if per_row else seq

    def vecs(mvec, n):
        out = []
        for part in jnp.split(mvec, n, axis=-1):
            if per_row:
                out.append(jnp.repeat(part, seq, axis=0).reshape(1, m, D_MODEL))
            else:
                out.append(part.reshape(batch, 1, D_MODEL))
        return out

    sh1, sc1, g1, sh2, sc2, g2 = vecs(mod[0], 6)
    sh1b, sc1b, g1b, sh2b, sc2b, g2b = vecs(mod[1], 6)
    shift_kv, scale_kv = vecs(mod_kv, 2)
    common = dict(tm=tm_ln, tk=512, rows_per_group=rows_per_group)

    h = _modulate(x, sc1, sh1, tm=tm_ln, rows_per_group=rows_per_group)
    proj = _matmul(h, a_w_in[0], tm=tm_wide, tn=512, name="hgrn_in_proj")
    o, s_fin = _hgrn(proj, lb_all[0], a_norm_g[0], s0[0], batch=batch, seq=seq,
                     out_dtype=BF16 if seq >= HEAD else F32)
    x, (h,) = _matmul_ln(o, a_w_out[0], x, g1, ln_g[0, 0], ln_b[0, 0], [(sc2, sh2)],
                         name="hgrn_out_proj", **common)
    hid = _matmul_pair(h, ffn_w_in[0], swiglu=True, tm=tm_wide, tn=512, name="swiglu_in")
    x, (h, z) = _matmul_ln(hid, ffn_w_out[0], x, g2, ln_g[0, 1], ln_b[0, 1],
                           [(sc1b, sh1b), (scale_kv, shift_kv)], name="ffn0_out_proj", **common)

    k, v = _matmul_pair(z, w_kv, swiglu=False, tm=tm_wide, tn=512, name="kv_proj")
    q = _matmul(h, b_w_q[0], scale=HEAD ** -0.5, tm=tm_wide, tn=512, name="q_proj")
    o = attend(q, k, v)
    x, (h,) = _matmul_ln(o, b_w_o[0], x, g1b, ln_g[1, 0], ln_b[1, 0], [(sc2b, sh2b)],
                         name="moba_out_proj", **common)
    hid = _matmul_pair(h, ffn_w_in[1], swiglu=True, tm=tm_wide, tn=512, name="swiglu_in")
    y, _ = _matmul_ln(hid, ffn_w_out[1], x, g2b, ln_g[1, 1], ln_b[1, 1], [],
                      name="ffn1_out_proj", **common)

    kv_shape = (batch, seq, N_HEADS, HEAD)
    return y.reshape(batch, seq, D_MODEL), s_fin[None], k.reshape(kv_shape), v.reshape(kv_shape)


def kernel(x_prompt, x_sample, state_hgrn, cache_k, cache_v, page_table, c_prompt, c_sample,
           a_w_in, a_lb_logits, a_norm_g, a_w_out, b_w_q, b_w_o, w_ada_kv, b_ada_kv, w_kv,
           w_ada, b_ada, ln_g, ln_b, ffn_w_in, ffn_w_out):
    assert a_w_in.shape[0] == 1 and b_w_q.shape[0] == 1 and w_ada.shape[0] == DEPTH
    bp, tp, _ = x_prompt.shape
    bs, tsamp, _ = x_sample.shape

    lb_all = _lower_bounds(a_lb_logits.astype(F32))

    c_all = jnp.concatenate([c_prompt, c_sample], axis=0)
    n_c = c_all.shape[0]
    c_rows = -(-n_c // BF16_ROWS) * BF16_ROWS
    c_all = jnp.pad(c_all, ((0, c_rows - n_c), (0, 0)))
    ada = functools.partial(_matmul, silu_in=True, tm=c_rows, tn=1024)
    mod = [ada(c_all, w_ada[l], bias=b_ada[l], name="ada_layer") for l in range(DEPTH)]
    mod_kv = ada(c_all, w_ada_kv, bias=b_ada_kv, name="ada_kv")

    weights = (a_w_in, lb_all, a_norm_g, a_w_out, b_w_q, b_w_o, w_kv, ln_g, ln_b, ffn_w_in, ffn_w_out)

    s0_prompt = jnp.zeros((a_w_in.shape[0], bp, N_HEADS, HEAD, HEAD), state_hgrn.dtype)
    y_p, s_p, k_p, v_p = _trunk(
        x_prompt, [mm[:bp] for mm in mod], mod_kv[:bp], s0_prompt,
        functools.partial(_moba_seq, batch=bp, seq=tp), weights, batch=bp, seq=tp)
    y_s, s_s, k_s, v_s = _trunk(
        x_sample, [mm[bp:bp + bs] for mm in mod], mod_kv[bp:bp + bs], state_hgrn,
        lambda q, k, v: _moba_paged(q, k, v, cache_k, cache_v, page_table, batch=bs, tq=tsamp),
        weights, batch=bs, seq=tsamp)
    return (y_p, y_s, s_p, s_s, k_p, v_p, k_s, v_s)
```

```python
import functools

import numpy as np
import jax
import jax.numpy as jnp
from jax import lax
from jax.experimental import pallas as pl
from jax.experimental.pallas import tpu as pltpu

F32 = jnp.float32
BF16 = jnp.bfloat16

D_MODEL = 2048
DEPTH = 2
HEAD = 128
N_HEADS = D_MODEL // HEAD
HGRN_PAD_CHUNK = 16
HGRN_HEADS_PER_STEP = 4
MOBA_BLOCK = 256
MOBA_TOPK = 3
PAGE_SIZE = 128
PAGES_PER_BLOCK = MOBA_BLOCK // PAGE_SIZE
BF16_ROWS = 16
LANES = 128
LN_EPS = 1e-5
RMS_EPS = 1e-6
ALPHA = (2.0 * DEPTH) ** 0.25
VMEM_LIMIT_BYTES = 56 * 1024 * 1024

NEG_INF = float("-inf")


def _params():
    return pltpu.CompilerParams(vmem_limit_bytes=VMEM_LIMIT_BYTES)


def _dot(a, b, precision=None):
    return lax.dot_general(a, b, (((1,), (0,)), ((), ())), preferred_element_type=F32,
                           precision=precision)


def _dot_nt(a, b, precision=None):
    return lax.dot_general(a, b, (((1,), (1,)), ((), ())), preferred_element_type=F32,
                           precision=precision)


def _dot_tn(a, b):
    return lax.dot_general(a, b, (((0,), (0,)), ((), ())), preferred_element_type=F32)


def _silu(x):
    return x * jax.nn.sigmoid(x)


def _weight_spec(w, layer, rows, cols, index_map):
    if w.ndim == 2:
        return pl.BlockSpec((rows, cols), index_map)
    return pl.BlockSpec((None, rows, cols), lambda *g: (layer,) + tuple(index_map(*g)))


def _lb_body(logit_ref, out_ref):
    z = logit_ref[...]
    n = z.shape[0]
    m = z[0:1]
    for r in range(1, n):
        m = jnp.maximum(m, z[r:r + 1])
    e = jnp.exp(z - m)
    tot = e[0:1]
    for r in range(1, n):
        tot = tot + e[r:r + 1]
    p = e / tot
    run = p[0:1]
    out_ref[0:1, :] = run
    for r in range(1, n):
        run = run + p[r:r + 1]
        out_ref[r:r + 1, :] = run


def _lower_bounds(logits):
    return pl.pallas_call(
        _lb_body, out_shape=jax.ShapeDtypeStruct(logits.shape, F32), name="lower_bounds")(logits)


def _mm_body(*refs, silu_in, scale, has_bias):
    if has_bias:
        x_ref, w_ref, b_ref, o_ref, wb_ref = refs
    else:
        x_ref, w_ref, o_ref, wb_ref = refs

    @pl.when(pl.program_id(1) == 0)
    def _():
        wb_ref[...] = w_ref[...].astype(BF16)

    x = x_ref[...]
    if silu_in:
        x = _silu(x)
    acc = _dot(x.astype(BF16), wb_ref[...])
    if has_bias:
        acc = acc + b_ref[...]
    if scale is not None:
        acc = acc * scale
    o_ref[...] = acc.astype(o_ref.dtype)


def _matmul(x, w, *, layer=0, bias=None, silu_in=False, scale=None, tm, tn, name):
    m, k = x.shape
    n = w.shape[-1]
    in_specs = [pl.BlockSpec((tm, k), lambda j, i: (i, 0)),
                _weight_spec(w, layer, k, tn, lambda j, i: (0, j))]
    args = [x, w]
    if bias is not None:
        in_specs.append(pl.BlockSpec((1, tn), lambda j, i: (0, j)))
        args.append(bias.reshape(1, n))
    return pl.pallas_call(
        functools.partial(_mm_body, silu_in=silu_in, scale=scale, has_bias=bias is not None),
        grid=(n // tn, m // tm),
        in_specs=in_specs,
        out_specs=pl.BlockSpec((tm, tn), lambda j, i: (i, j)),
        out_shape=jax.ShapeDtypeStruct((m, n), F32),
        scratch_shapes=[pltpu.VMEM((k, tn), BF16)],
        compiler_params=_params(),
        name=name,
    )(*args)


def _pair_body(*refs, swiglu):
    if swiglu:
        x_ref, wa_ref, wu_ref, o_ref, wab_ref, wub_ref = refs
    else:
        x_ref, wa_ref, wu_ref, oa_ref, ou_ref, wab_ref, wub_ref = refs

    @pl.when(pl.program_id(1) == 0)
    def _():
        wab_ref[...] = wa_ref[...].astype(BF16)
        wub_ref[...] = wu_ref[...].astype(BF16)

    x = x_ref[...]
    a = _dot(x, wab_ref[...])
    u = _dot(x, wub_ref[...])
    if swiglu:
        o_ref[...] = (_silu(a) * u).astype(o_ref.dtype)
    else:
        oa_ref[...] = a
        ou_ref[...] = u


def _matmul_pair(x, w, *, layer=0, swiglu, tm, tn, name):
    m, k = x.shape
    f = w.shape[-1] // 2
    nj = f // tn
    out_spec = pl.BlockSpec((tm, tn), lambda j, i: (i, j))
    if swiglu:
        out_specs, out_shape = out_spec, jax.ShapeDtypeStruct((m, f), BF16)
    else:
        out_specs, out_shape = [out_spec] * 2, [jax.ShapeDtypeStruct((m, f), F32)] * 2
    return pl.pallas_call(
        functools.partial(_pair_body, swiglu=swiglu),
        grid=(nj, m // tm),
        in_specs=[pl.BlockSpec((tm, k), lambda j, i: (i, 0)),
                  _weight_spec(w, layer, k, tn, lambda j, i: (0, j)),
                  _weight_spec(w, layer, k, tn, lambda j, i: (0, j + nj))],
        out_specs=out_specs,
        out_shape=out_shape,
        scratch_shapes=[pltpu.VMEM((k, tn), BF16), pltpu.VMEM((k, tn), BF16)],
        compiler_params=_params(),
        name=name,
    )(x, w, w)


def _modulate_body(x_ref, sc_ref, sh_ref, o_ref):
    o_ref[...] = (x_ref[...] * (1.0 + sc_ref[...]) + sh_ref[...]).astype(o_ref.dtype)


def _mod_spec(vec, tm, rows_per_group):
    r = vec.shape[1]
    tiles = rows_per_group // tm
    return pl.BlockSpec((None, r, D_MODEL), lambda i, *_: (i // tiles, 0, 0))


def _modulate(x, sc, sh, *, tm, rows_per_group):
    m = x.shape[0]
    return pl.pallas_call(
        _modulate_body,
        grid=(m // tm,),
        in_specs=[pl.BlockSpec((tm, D_MODEL), lambda i: (i, 0)),
                  _mod_spec(sc, tm, rows_per_group), _mod_spec(sh, tm, rows_per_group)],
        out_specs=pl.BlockSpec((tm, D_MODEL), lambda i: (i, 0)),
        out_shape=jax.ShapeDtypeStruct((m, D_MODEL), BF16),
        compiler_params=_params(),
        name="modulate",
    )(x, sc, sh)


def _mm_ln_body(*refs, n_mod, nk):
    a_ref, w_ref, x_ref, gate_ref, lng_ref, lnb_ref = refs[:6]
    mod_refs = refs[6:6 + 2 * n_mod]
    y_ref = refs[6 + 2 * n_mod]
    h_refs = refs[7 + 2 * n_mod:7 + 3 * n_mod]
    acc_ref = refs[7 + 3 * n_mod]
    kk = pl.program_id(1)

    @pl.when(kk == 0)
    def _():
        acc_ref[...] = jnp.zeros_like(acc_ref)

    acc_ref[...] += _dot(a_ref[...].astype(BF16), w_ref[...].astype(BF16))

    @pl.when(kk == nk - 1)
    def _():
        r = ALPHA * x_ref[...] + (1.0 + gate_ref[...]) * acc_ref[...]
        mu = jnp.mean(r, axis=-1, keepdims=True)
        d = r - mu
        var = jnp.mean(d * d, axis=-1, keepdims=True)
        y = d * lax.rsqrt(var + LN_EPS) * lng_ref[...] + lnb_ref[...]
        y_ref[...] = y
        for j in range(n_mod):
            sc_ref, sh_ref = mod_refs[2 * j], mod_refs[2 * j + 1]
            h_refs[j][...] = (y * (1.0 + sc_ref[...]) + sh_ref[...]).astype(BF16)


def _matmul_ln(a, w, x, gate, ln_g, ln_b, mods, *, layer=0, tm, tk, rows_per_group, name):
    m, k = a.shape
    nk = k // tk
    n_mod = len(mods)
    row_spec = pl.BlockSpec((tm, D_MODEL), lambda i, kk: (i, 0))
    vec_spec = pl.BlockSpec((1, D_MODEL), lambda i, kk: (0, 0))
    in_specs = [pl.BlockSpec((tm, tk), lambda i, kk: (i, kk)),
                _weight_spec(w, layer, tk, D_MODEL, lambda i, kk: (kk, 0)),
                row_spec, _mod_spec(gate, tm, rows_per_group), vec_spec, vec_spec]
    args = [a, w, x, gate, ln_g.reshape(1, D_MODEL), ln_b.reshape(1, D_MODEL)]
    for sc, sh in mods:
        in_specs += [_mod_spec(sc, tm, rows_per_group), _mod_spec(sh, tm, rows_per_group)]
        args += [sc, sh]
    out = pl.pallas_call(
        functools.partial(_mm_ln_body, n_mod=n_mod, nk=nk),
        grid=(m // tm, nk),
        in_specs=in_specs,
        out_specs=[row_spec] * (1 + n_mod),
        out_shape=[jax.ShapeDtypeStruct((m, D_MODEL), F32)]
        + [jax.ShapeDtypeStruct((m, D_MODEL), BF16)] * n_mod,
        scratch_shapes=[pltpu.VMEM((tm, D_MODEL), F32)],
        compiler_params=_params(),
        name=name,
    )(*args)
    return out[0], list(out[1:])


def _hgrn_tables(chunk):
    levels = chunk.bit_length() - 1
    t = np.arange(chunk)[:, None]
    r = np.arange(chunk)[None, :]
    mats = [(r <= t), (r > t)]
    for lv in range(levels):
        size = 1 << lv
        odd = ((t >> lv) & 1) == 1
        start = t & ~(size - 1)
        end = t | (size - 1)
        mats.append(np.where(odd, (r >= start) & (r <= t), (r > t) & (r <= end)))
    tab = np.concatenate(mats, axis=0).astype(np.float32)
    return np.concatenate([tab] * 3, axis=1), levels


def _hgrn_body(q_ref, f_ref, i_ref, g_ref, lb_ref, ng_ref, s0_ref, tab_ref,
               o_ref, sfin_ref, st_ref, *, chunk, valid, levels, n_chunks, heads):
    c = pl.program_id(2)
    hs = range(heads)
    cols = [slice(i * HEAD, (i + 1) * HEAD) for i in hs]

    @pl.when(c == 0)
    def _():
        for i in hs:
            st_ref[i] = s0_ref[i].T

    def pad(a):
        if valid == chunk:
            return a
        return jnp.concatenate([a, jnp.zeros((chunk - valid, HEAD), F32)], axis=0)

    lf, k, q, vb = [], [], [], []
    for i in hs:
        lb = lb_ref[:, cols[i]]
        f = lb + (1.0 - lb) * jax.nn.sigmoid(f_ref[:, cols[i]])
        lf.append(pad(jnp.log(f)))
        k.append(pad(1.0 - f))
        q.append(pad(q_ref[:, cols[i]]))
        vb.append(pad(i_ref[:, cols[i]]).astype(BF16))

    pieces = []
    for i in hs:
        hi = lf[i].astype(BF16)
        r1 = lf[i] - hi.astype(F32)
        mid = r1.astype(BF16)
        lo = (r1 - mid.astype(F32)).astype(BF16)
        pieces.append(jnp.concatenate([hi, mid, lo], axis=0))
    e_all = _dot(tab_ref[...], jnp.concatenate(pieces, axis=1))

    def table(t, i):
        return e_all[t * chunk:(t + 1) * chunk, cols[i]]

    row = lax.broadcasted_iota(jnp.int32, (chunk, chunk), 0)
    col = lax.broadcasted_iota(jnp.int32, (chunk, chunk), 1)
    trow = lax.broadcasted_iota(jnp.int32, (chunk, 1), 0)

    attn = [jnp.where(row == col, _dot_nt(q[i].astype(BF16), k[i].astype(BF16)), 0.0) for i in hs]
    for lv in range(levels):
        odd = ((trow >> lv) & 1) == 1
        pair = ((row >> (lv + 1)) == (col >> (lv + 1))) & (((row >> lv) & 1) == 1) \
            & (((col >> lv) & 1) == 0)
        for i in hs:
            ex = jnp.exp(table(2 + lv, i))
            qs = jnp.where(odd, q[i] * ex, 0.0).astype(BF16)
            ks = jnp.where(odd, 0.0, k[i] * ex).astype(BF16)
            attn[i] = attn[i] + jnp.where(pair, _dot_nt(qs, ks), 0.0)

    for i in hs:
        b = table(0, i)
        st = st_ref[i]
        o = _dot_nt((q[i] * jnp.exp(b)).astype(BF16), st.astype(BF16)) \
            + _dot(attn[i].astype(BF16), vb[i])
        b_last = b[chunk - 1:chunk]
        st_new = jnp.exp(b_last) * st + _dot_tn(vb[i], (k[i] * jnp.exp(table(1, i))).astype(BF16))
        st_ref[i] = st_new

        o = o[0:valid]
        o = o * lax.rsqrt(jnp.mean(o * o, axis=-1, keepdims=True) + RMS_EPS) * ng_ref[:, cols[i]]
        o_ref[:, cols[i]] = (o * _silu(g_ref[:, cols[i]])).astype(o_ref.dtype)

    @pl.when(c == n_chunks - 1)
    def _():
        for i in hs:
            sfin_ref[i] = st_ref[i].T


def _hgrn(proj, lb, norm_g, s0, *, batch, seq, out_dtype):
    if seq >= HEAD:
        chunk = valid = HEAD
        heads = HGRN_HEADS_PER_STEP
    else:
        chunk, valid = HGRN_PAD_CHUNK, seq
        heads = N_HEADS
    n_chunks = seq // valid
    groups = N_HEADS // heads
    tables, levels = _hgrn_tables(chunk)
    tables = jnp.asarray(tables, BF16)
    width = heads * HEAD

    def col_spec(part):
        return pl.BlockSpec((valid, width), lambda b, h, c: (b * n_chunks + c, part * groups + h))

    head_vec = pl.BlockSpec((1, width), lambda b, h, c: (0, h))
    state_spec = pl.BlockSpec((None, heads, HEAD, HEAD), lambda b, h, c: (b, h, 0, 0))
    o, s_fin = pl.pallas_call(
        functools.partial(_hgrn_body, chunk=chunk, valid=valid, levels=levels,
                          n_chunks=n_chunks, heads=heads),
        grid=(batch, groups, n_chunks),
        in_specs=[col_spec(0), col_spec(1), col_spec(2), col_spec(3), head_vec, head_vec,
                  state_spec, pl.BlockSpec(tables.shape, lambda b, h, c: (0, 0))],
        out_specs=[pl.BlockSpec((valid, width), lambda b, h, c: (b * n_chunks + c, h)), state_spec],
        out_shape=[jax.ShapeDtypeStruct((batch * seq, D_MODEL), out_dtype),
                   jax.ShapeDtypeStruct((batch, N_HEADS, HEAD, HEAD), F32)],
        scratch_shapes=[pltpu.VMEM((heads, HEAD, HEAD), F32)],
        compiler_params=_params(),
        name="hgrn",
    )(proj, proj, proj, proj, lb.reshape(1, D_MODEL), norm_g.reshape(1, D_MODEL), s0, tables)
    return o, s_fin


def _select_bias(gate, n_valid_blocks, axis):
    nb = gate.shape[axis]
    blk = lax.broadcasted_iota(jnp.int32, gate.shape, axis)
    valid = blk < n_valid_blocks
    beaten = jnp.zeros(gate.shape, jnp.int32)
    for m in range(nb):
        gm = gate[m:m + 1, :] if axis == 0 else gate[:, m:m + 1]
        wins = (gm > gate) | ((gm == gate) & (m < blk))
        beaten = beaten + jnp.where(wins, 1, 0) * jnp.where(m < n_valid_blocks, 1, 0)
    return jnp.where(valid & (beaten < MOBA_TOPK), 0.0, NEG_INF)


def _moba_seq_body(q_ref, k_ref, v_ref, o_ref, kb_ref, vt_ref, kmean_ref, s_ref, *, nb):
    i = pl.program_id(2)

    @pl.when(i == 0)
    def _():
        kf = k_ref[...]
        kb_ref[...] = kf.astype(BF16)
        vt_ref[...] = v_ref[...].T.astype(BF16)
        means = [jnp.mean(kf[n * MOBA_BLOCK:(n + 1) * MOBA_BLOCK], axis=0, keepdims=True)
                 for n in range(nb)]
        kmean_ref[...] = jnp.concatenate(means, axis=0)

    qt = q_ref[...].T
    qtb = qt.astype(BF16)
    gate = _dot(kmean_ref[...], qt, precision=lax.Precision.HIGHEST)
    blk = lax.broadcasted_iota(jnp.int32, gate.shape, 0)
    bias = jnp.where(blk == i, 0.0, _select_bias(gate, i, axis=0))

    s_all = _dot(kb_ref[...], qtb)
    for n in range(nb):
        s_ref[n] = s_all[n * MOBA_BLOCK:(n + 1) * MOBA_BLOCK]
    key = lax.broadcasted_iota(jnp.int32, (MOBA_BLOCK, MOBA_BLOCK), 0)
    qry = lax.broadcasted_iota(jnp.int32, (MOBA_BLOCK, MOBA_BLOCK), 1)
    s_ref[i] = jnp.where(key <= qry, s_ref[i], NEG_INF)

    m = jnp.max(s_ref[0] + bias[0:1], axis=0, keepdims=True)
    for n in range(1, nb):
        m = jnp.maximum(m, jnp.max(s_ref[n] + bias[n:n + 1], axis=0, keepdims=True))
    l = jnp.zeros_like(m)
    ps = []
    for n in range(nb):
        p = jnp.exp(s_ref[n] + bias[n:n + 1] - m)
        l = l + jnp.sum(p, axis=0, keepdims=True)
        ps.append(p.astype(BF16))
    acc = _dot(vt_ref[...], jnp.concatenate(ps, axis=0))
    o_ref[...] = (acc / l).T.astype(o_ref.dtype)


def _moba_seq(q, k, v, *, batch, seq):
    nb = seq // MOBA_BLOCK
    return pl.pallas_call(
        functools.partial(_moba_seq_body, nb=nb),
        grid=(batch, N_HEADS, nb),
        in_specs=[pl.BlockSpec((MOBA_BLOCK, HEAD), lambda b, h, i: (b * nb + i, h)),
                  pl.BlockSpec((seq, HEAD), lambda b, h, i: (b, h)),
                  pl.BlockSpec((seq, HEAD), lambda b, h, i: (b, h))],
        out_specs=pl.BlockSpec((MOBA_BLOCK, HEAD), lambda b, h, i: (b * nb + i, h)),
        out_shape=jax.ShapeDtypeStruct((batch * seq, D_MODEL), BF16),
        scratch_shapes=[pltpu.VMEM((seq, HEAD), BF16), pltpu.VMEM((HEAD, seq), BF16),
                        pltpu.VMEM((nb, HEAD), F32),
                        pltpu.VMEM((nb, MOBA_BLOCK, MOBA_BLOCK), F32)],
        compiler_params=_params(),
        name="moba_prompt",
    )(q, k, v)


def _moba_paged_body(pt_ref, q_ref, kn_ref, vn_ref, k0_ref, k1_ref, v0_ref, v1_ref, o_ref,
                     qs_ref, hmask_ref, kmean_ref, m_ref, l_ref, part_ref, *, nb, tq):
    del pt_ref
    n = pl.program_id(1)
    rows = N_HEADS * tq
    page_rows = PAGE_SIZE * N_HEADS
    head_shift = N_HEADS.bit_length() - 1
    tq_shift = tq.bit_length() - 1

    @pl.when(n == 0)
    def _():
        q = q_ref[...]
        qs_ref[...] = jnp.concatenate([q[:, h * HEAD:(h + 1) * HEAD] for h in range(N_HEADS)],
                                      axis=0)
        r = lax.broadcasted_iota(jnp.int32, (rows, page_rows), 0)
        cc = lax.broadcasted_iota(jnp.int32, (rows, page_rows), 1)
        hmask_ref[...] = jnp.where((cc & (N_HEADS - 1)) == (r >> tq_shift), 0.0, NEG_INF)
        m_ref[...] = jnp.zeros_like(m_ref)
        l_ref[...] = jnp.zeros_like(l_ref)

    qsb = qs_ref[...].astype(BF16)
    hmask = hmask_ref[...]
    k_pages = (k0_ref, k1_ref)
    v_pages = (v0_ref, v1_ref)

    ksum = jnp.sum(k0_ref[...], axis=0) + jnp.sum(k1_ref[...], axis=0)
    kmean_ref[n] = ksum * (1.0 / MOBA_BLOCK)

    s = [_dot_nt(qsb, kp[...].reshape(page_rows, HEAD).astype(BF16)) + hmask for kp in k_pages]
    m_n = jnp.max(s[0], axis=-1, keepdims=True)
    for sp in s[1:]:
        m_n = jnp.maximum(m_n, jnp.max(sp, axis=-1, keepdims=True))
    l_n = jnp.zeros_like(m_n)
    o_n = jnp.zeros((rows, HEAD), F32)
    for sp, vp in zip(s, v_pages):
        p = jnp.exp(sp - m_n)
        l_n = l_n + jnp.sum(p, axis=-1, keepdims=True)
        o_n = o_n + _dot(p.astype(BF16), vp[...].reshape(page_rows, HEAD).astype(BF16))
    part_ref[n] = o_n
    lane = lax.broadcasted_iota(jnp.int32, (rows, LANES), 1)
    m_ref[...] = jnp.where(lane == n, m_n, m_ref[...])
    l_ref[...] = jnp.where(lane == n, l_n, l_ref[...])

    @pl.when(n == nb - 1)
    def _():
        qs = qs_ref[...]
        gates = [_dot_nt(qs[h * tq:(h + 1) * tq], kmean_ref[:, h, :], precision=lax.Precision.HIGHEST)
                 for h in range(N_HEADS)]
        bias = _select_bias(jnp.concatenate(gates, axis=0), nb, axis=1)

        s_own = _dot_nt(qsb, kn_ref[...].astype(BF16))
        r = lax.broadcasted_iota(jnp.int32, s_own.shape, 0)
        cc = lax.broadcasted_iota(jnp.int32, s_own.shape, 1)
        keep = ((cc & (N_HEADS - 1)) == (r >> tq_shift)) & ((cc >> head_shift) <= (r & (tq - 1)))
        s_own = jnp.where(keep, s_own, NEG_INF)
        m_own = jnp.max(s_own, axis=-1, keepdims=True)
        p_own = jnp.exp(s_own - m_own)
        l_own = jnp.sum(p_own, axis=-1, keepdims=True)
        o_own = _dot(p_own.astype(BF16), vn_ref[...].astype(BF16))

        mb = m_ref[:, 0:nb] + bias
        m = jnp.maximum(m_own, jnp.max(mb, axis=-1, keepdims=True))
        w = jnp.exp(mb - m)
        w_own = jnp.exp(m_own - m)
        l = w_own * l_own + jnp.sum(w * l_ref[:, 0:nb], axis=-1, keepdims=True)
        o = w_own * o_own
        for j in range(nb):
            o = o + w[:, j:j + 1] * part_ref[j]
        o = o / l
        for h in range(N_HEADS):
            o_ref[:, h * HEAD:(h + 1) * HEAD] = o[h * tq:(h + 1) * tq]


def _moba_paged(q, k_new, v_new, cache_k, cache_v, page_table, *, batch, tq):
    n_pages = page_table.shape[1]
    nb = n_pages // PAGES_PER_BLOCK
    rows = N_HEADS * tq
    assert nb <= LANES and tq & (tq - 1) == 0 and rows % 8 == 0
    page_block = (None, PAGE_SIZE, N_HEADS, HEAD)

    def page(p):
        return pl.BlockSpec(page_block, lambda b, n, pt: (pt[b, PAGES_PER_BLOCK * n + p], 0, 0, 0))

    q_spec = pl.BlockSpec((tq, D_MODEL), lambda b, n, pt: (b, 0))
    new_spec = pl.BlockSpec((rows, HEAD), lambda b, n, pt: (b, 0))
    grid_spec = pltpu.PrefetchScalarGridSpec(
        num_scalar_prefetch=1,
        grid=(batch, nb),
        in_specs=[q_spec, new_spec, new_spec, page(0), page(1), page(0), page(1)],
        out_specs=q_spec,
        scratch_shapes=[pltpu.VMEM((rows, HEAD), F32),
                        pltpu.VMEM((rows, PAGE_SIZE * N_HEADS), F32),
                        pltpu.VMEM((nb, N_HEADS, HEAD), F32),
                        pltpu.VMEM((rows, LANES), F32),
                        pltpu.VMEM((rows, LANES), F32),
                        pltpu.VMEM((nb, rows, HEAD), F32)],
    )
    k_rows = k_new.reshape(batch * rows, HEAD)
    v_rows = v_new.reshape(batch * rows, HEAD)
    return pl.pallas_call(
        functools.partial(_moba_paged_body, nb=nb, tq=tq),
        grid_spec=grid_spec,
        out_shape=jax.ShapeDtypeStruct((batch * tq, D_MODEL), F32),
        compiler_params=_params(),
        name="moba_paged",
    )(page_table, q, k_rows, v_rows, cache_k, cache_k, cache_v, cache_v)


def _tile_sizes(m):
    if m >= 1024:
        return dict(tm_wide=1024, tm_ln=512)
    return dict(tm_wide=m, tm_ln=m)


def _trunk(x, mod, mod_kv, s0, attend, weights, *, batch, seq):
    (a_w_in, lb_all, a_norm_g, a_w_out, b_w_q, b_w_o, w_kv, ln_g, ln_b, ffn_w_in, ffn_w_out) = weights
    m = batch * seq
    x = x.reshape(m, D_MODEL)
    ts = _tile_sizes(m)
    tm_wide, tm_ln = ts["tm_wide"], ts["tm_ln"]
    per_row = seq < tm_ln
    rows_per_group = m if per_row else seq

    def vecs(mvec, n):
        out = []
        for part in jnp.split(mvec, n, axis=-1):
            if per_row:
                out.append(jnp.repeat(part, seq, axis=0).reshape(1, m, D_MODEL))
            else:
                out.append(part.reshape(batch, 1, D_MODEL))
        return out

    sh1, sc1, g1, sh2, sc2, g2 = vecs(mod[0], 6)
    sh1b, sc1b, g1b, sh2b, sc2b, g2b = vecs(mod[1], 6)
    shift_kv, scale_kv = vecs(mod_kv, 2)
    common = dict(tm=tm_ln, tk=512, rows_per_group=rows_per_group)

    h = _modulate(x, sc1, sh1, tm=tm_ln, rows_per_group=rows_per_group)
    proj = _matmul(h, a_w_in, tm=tm_wide, tn=512, name="hgrn_in_proj")
    o, s_fin = _hgrn(proj, lb_all[0], a_norm_g[0], s0[0], batch=batch, seq=seq,
                     out_dtype=BF16 if seq >= HEAD else F32)
    x, (h,) = _matmul_ln(o, a_w_out, x, g1, ln_g[0, 0], ln_b[0, 0], [(sc2, sh2)],
                         name="hgrn_out_proj", **common)
    hid = _matmul_pair(h, ffn_w_in, layer=0, swiglu=True, tm=tm_wide, tn=512, name="swiglu_in")
    x, (h, z) = _matmul_ln(hid, ffn_w_out, x, g2, ln_g[0, 1], ln_b[0, 1],
                           [(sc1b, sh1b), (scale_kv, shift_kv)], layer=0,
                           name="ffn0_out_proj", **common)

    k, v = _matmul_pair(z, w_kv, swiglu=False, tm=tm_wide, tn=512, name="kv_proj")
    q = _matmul(h, b_w_q, scale=HEAD ** -0.5, tm=tm_wide, tn=512, name="q_proj")
    o = attend(q, k, v)
    x, (h,) = _matmul_ln(o, b_w_o, x, g1b, ln_g[1, 0], ln_b[1, 0], [(sc2b, sh2b)],
                         name="moba_out_proj", **common)
    hid = _matmul_pair(h, ffn_w_in, layer=1, swiglu=True, tm=tm_wide, tn=512, name="swiglu_in")
    y, _ = _matmul_ln(hid, ffn_w_out, x, g2b, ln_g[1, 1], ln_b[1, 1], [], layer=1,
                      name="ffn1_out_proj", **common)

    kv_shape = (batch, seq, N_HEADS, HEAD)
    return y.reshape(batch, seq, D_MODEL), s_fin[None], k.reshape(kv_shape), v.reshape(kv_shape)


def kernel(x_prompt, x_sample, state_hgrn, cache_k, cache_v, page_table, c_prompt, c_sample,
           a_w_in, a_lb_logits, a_norm_g, a_w_out, b_w_q, b_w_o, w_ada_kv, b_ada_kv, w_kv,
           w_ada, b_ada, ln_g, ln_b, ffn_w_in, ffn_w_out):
    assert a_w_in.shape[0] == 1 and b_w_q.shape[0] == 1 and w_ada.shape[0] == DEPTH
    bp, tp, _ = x_prompt.shape
    bs, tsamp, _ = x_sample.shape

    lb_all = _lower_bounds(a_lb_logits.astype(F32))

    c_all = jnp.concatenate([c_prompt, c_sample], axis=0)
    n_c = c_all.shape[0]
    c_rows = -(-n_c // BF16_ROWS) * BF16_ROWS
    c_all = jnp.pad(c_all, ((0, c_rows - n_c), (0, 0)))
    ada = functools.partial(_matmul, silu_in=True, tm=c_rows, tn=1024)
    mod = [ada(c_all, w_ada, layer=l, bias=b_ada[l], name="ada_layer") for l in range(DEPTH)]
    mod_kv = ada(c_all, w_ada_kv, bias=b_ada_kv, name="ada_kv")

    weights = (a_w_in, lb_all, a_norm_g, a_w_out, b_w_q, b_w_o, w_kv, ln_g, ln_b, ffn_w_in, ffn_w_out)

    s0_prompt = jnp.zeros((a_w_in.shape[0], bp, N_HEADS, HEAD, HEAD), state_hgrn.dtype)
    y_p, s_p, k_p, v_p = _trunk(
        x_prompt, [mm[:bp] for mm in mod], mod_kv[:bp], s0_prompt,
        functools.partial(_moba_seq, batch=bp, seq=tp), weights, batch=bp, seq=tp)
    y_s, s_s, k_s, v_s = _trunk(
        x_sample, [mm[bp:bp + bs] for mm in mod], mod_kv[bp:bp + bs], state_hgrn,
        lambda q, k, v: _moba_paged(q, k, v, cache_k, cache_v, page_table, batch=bs, tq=tsamp),
        weights, batch=bs, seq=tsamp)
    return (y_p, y_s, s_p, s_s, k_p, v_p, k_s, v_s)
```

```python
import functools

import numpy as np
import jax
import jax.numpy as jnp
from jax import lax
from jax.experimental import pallas as pl
from jax.experimental.pallas import tpu as pltpu

F32 = jnp.float32
BF16 = jnp.bfloat16

D_MODEL = 2048
DEPTH = 2
HEAD = 128
N_HEADS = D_MODEL // HEAD
HGRN_PAD_CHUNK = 16
HGRN_HEADS_PER_STEP = 8
MOBA_BLOCK = 256
MOBA_TOPK = 3
PAGE_SIZE = 128
PAGES_PER_BLOCK = MOBA_BLOCK // PAGE_SIZE
PAGED_BLOCKS_PER_STEP = 2
BF16_ROWS = 16
LANES = 128
LN_EPS = 1e-5
RMS_EPS = 1e-6
ALPHA = (2.0 * DEPTH) ** 0.25
VMEM_LIMIT_BYTES = 56 * 1024 * 1024

NEG_INF = float("-inf")


def _params():
    return pltpu.CompilerParams(vmem_limit_bytes=VMEM_LIMIT_BYTES)


def _dot(a, b, precision=None):
    return lax.dot_general(a, b, (((1,), (0,)), ((), ())), preferred_element_type=F32,
                           precision=precision)


def _dot_nt(a, b, precision=None):
    return lax.dot_general(a, b, (((1,), (1,)), ((), ())), preferred_element_type=F32,
                           precision=precision)


def _dot_tn(a, b):
    return lax.dot_general(a, b, (((0,), (0,)), ((), ())), preferred_element_type=F32)


def _silu(x):
    return x * jax.nn.sigmoid(x)


def _weight_spec(w, layer, rows, cols, index_map):
    if w.ndim == 2:
        return pl.BlockSpec((rows, cols), index_map)
    return pl.BlockSpec((None, rows, cols), lambda *g: (layer,) + tuple(index_map(*g)))


def _lb_body(logit_ref, out_ref):
    z = logit_ref[...]
    n = z.shape[0]
    m = z[0:1]
    for r in range(1, n):
        m = jnp.maximum(m, z[r:r + 1])
    e = jnp.exp(z - m)
    tot = e[0:1]
    for r in range(1, n):
        tot = tot + e[r:r + 1]
    p = e / tot
    run = p[0:1]
    out_ref[0:1, :] = run
    for r in range(1, n):
        run = run + p[r:r + 1]
        out_ref[r:r + 1, :] = run


def _lower_bounds(logits):
    return pl.pallas_call(
        _lb_body, out_shape=jax.ShapeDtypeStruct(logits.shape, F32), name="lower_bounds")(logits)


def _mm_body(*refs, silu_in, scale, has_bias):
    if has_bias:
        x_ref, w_ref, b_ref, o_ref, wb_ref = refs
    else:
        x_ref, w_ref, o_ref, wb_ref = refs

    @pl.when(pl.program_id(1) == 0)
    def _():
        wb_ref[...] = w_ref[...].astype(BF16)

    x = x_ref[...]
    if silu_in:
        x = _silu(x)
    acc = _dot(x.astype(BF16), wb_ref[...])
    if has_bias:
        acc = acc + b_ref[...]
    if scale is not None:
        acc = acc * scale
    o_ref[...] = acc.astype(o_ref.dtype)


def _matmul(x, w, *, layer=0, bias=None, silu_in=False, scale=None, tm, tn, name):
    m, k = x.shape
    n = w.shape[-1]
    in_specs = [pl.BlockSpec((tm, k), lambda j, i: (i, 0)),
                _weight_spec(w, layer, k, tn, lambda j, i: (0, j))]
    args = [x, w]
    if bias is not None:
        in_specs.append(pl.BlockSpec((1, tn), lambda j, i: (0, j)))
        args.append(bias.reshape(1, n))
    return pl.pallas_call(
        functools.partial(_mm_body, silu_in=silu_in, scale=scale, has_bias=bias is not None),
        grid=(n // tn, m // tm),
        in_specs=in_specs,
        out_specs=pl.BlockSpec((tm, tn), lambda j, i: (i, j)),
        out_shape=jax.ShapeDtypeStruct((m, n), F32),
        scratch_shapes=[pltpu.VMEM((k, tn), BF16)],
        compiler_params=_params(),
        name=name,
    )(*args)


def _pair_body(*refs, swiglu):
    if swiglu:
        x_ref, wa_ref, wu_ref, o_ref, wab_ref, wub_ref = refs
    else:
        x_ref, wa_ref, wu_ref, oa_ref, ou_ref, wab_ref, wub_ref = refs

    @pl.when(pl.program_id(1) == 0)
    def _():
        wab_ref[...] = wa_ref[...].astype(BF16)
        wub_ref[...] = wu_ref[...].astype(BF16)

    x = x_ref[...]
    a = _dot(x, wab_ref[...])
    u = _dot(x, wub_ref[...])
    if swiglu:
        o_ref[...] = (_silu(a) * u).astype(o_ref.dtype)
    else:
        oa_ref[...] = a
        ou_ref[...] = u


def _matmul_pair(x, w, *, layer=0, swiglu, tm, tn, name):
    m, k = x.shape
    f = w.shape[-1] // 2
    nj = f // tn
    out_spec = pl.BlockSpec((tm, tn), lambda j, i: (i, j))
    if swiglu:
        out_specs, out_shape = out_spec, jax.ShapeDtypeStruct((m, f), BF16)
    else:
        out_specs, out_shape = [out_spec] * 2, [jax.ShapeDtypeStruct((m, f), F32)] * 2
    return pl.pallas_call(
        functools.partial(_pair_body, swiglu=swiglu),
        grid=(nj, m // tm),
        in_specs=[pl.BlockSpec((tm, k), lambda j, i: (i, 0)),
                  _weight_spec(w, layer, k, tn, lambda j, i: (0, j)),
                  _weight_spec(w, layer, k, tn, lambda j, i: (0, j + nj))],
        out_specs=out_specs,
        out_shape=out_shape,
        scratch_shapes=[pltpu.VMEM((k, tn), BF16), pltpu.VMEM((k, tn), BF16)],
        compiler_params=_params(),
        name=name,
    )(x, w, w)


def _modulate_body(x_ref, sc_ref, sh_ref, o_ref):
    o_ref[...] = (x_ref[...] * (1.0 + sc_ref[...]) + sh_ref[...]).astype(o_ref.dtype)


def _mod_spec(vec, tm, rows_per_group):
    r = vec.shape[1]
    tiles = rows_per_group // tm
    return pl.BlockSpec((None, r, D_MODEL), lambda i, *_: (i // tiles, 0, 0))


def _modulate(x, sc, sh, *, tm, rows_per_group):
    m = x.shape[0]
    return pl.pallas_call(
        _modulate_body,
        grid=(m // tm,),
        in_specs=[pl.BlockSpec((tm, D_MODEL), lambda i: (i, 0)),
                  _mod_spec(sc, tm, rows_per_group), _mod_spec(sh, tm, rows_per_group)],
        out_specs=pl.BlockSpec((tm, D_MODEL), lambda i: (i, 0)),
        out_shape=jax.ShapeDtypeStruct((m, D_MODEL), BF16),
        compiler_params=_params(),
        name="modulate",
    )(x, sc, sh)


def _ln_body(*refs, n_mod):
    mix_ref, x_ref, gate_ref, lng_ref, lnb_ref = refs[:5]
    mod_refs = refs[5:5 + 2 * n_mod]
    y_ref = refs[5 + 2 * n_mod]
    h_refs = refs[6 + 2 * n_mod:]
    r = ALPHA * x_ref[...] + (1.0 + gate_ref[...]) * mix_ref[...]
    mu = jnp.mean(r, axis=-1, keepdims=True)
    d = r - mu
    var = jnp.mean(d * d, axis=-1, keepdims=True)
    y = d * lax.rsqrt(var + LN_EPS) * lng_ref[...] + lnb_ref[...]
    y_ref[...] = y
    for j in range(n_mod):
        sc_ref, sh_ref = mod_refs[2 * j], mod_refs[2 * j + 1]
        h_refs[j][...] = (y * (1.0 + sc_ref[...]) + sh_ref[...]).astype(BF16)


def _matmul_ln(a, w, x, gate, ln_g, ln_b, mods, *, layer=0, tm_mm, tn, tm, rows_per_group, name):
    mix = _matmul(a, w, layer=layer, tm=tm_mm, tn=tn, name=name)
    m = x.shape[0]
    n_mod = len(mods)
    row_spec = pl.BlockSpec((tm, D_MODEL), lambda i: (i, 0))
    vec_spec = pl.BlockSpec((1, D_MODEL), lambda i: (0, 0))
    in_specs = [row_spec, row_spec, _mod_spec(gate, tm, rows_per_group), vec_spec, vec_spec]
    args = [mix, x, gate, ln_g.reshape(1, D_MODEL), ln_b.reshape(1, D_MODEL)]
    for sc, sh in mods:
        in_specs += [_mod_spec(sc, tm, rows_per_group), _mod_spec(sh, tm, rows_per_group)]
        args += [sc, sh]
    out = pl.pallas_call(
        functools.partial(_ln_body, n_mod=n_mod),
        grid=(m // tm,),
        in_specs=in_specs,
        out_specs=[row_spec] * (1 + n_mod),
        out_shape=[jax.ShapeDtypeStruct((m, D_MODEL), F32)]
        + [jax.ShapeDtypeStruct((m, D_MODEL), BF16)] * n_mod,
        compiler_params=_params(),
        name="residual_ln",
    )(*args)
    return out[0], list(out[1:])


def _hgrn_tables(chunk):
    levels = chunk.bit_length() - 1
    t = np.arange(chunk)[:, None]
    r = np.arange(chunk)[None, :]
    mats = [(r <= t), (r > t)]
    for lv in range(levels):
        size = 1 << lv
        odd = ((t >> lv) & 1) == 1
        start = t & ~(size - 1)
        end = t | (size - 1)
        mats.append(np.where(odd, (r >= start) & (r <= t), (r > t) & (r <= end)))
    tab = np.concatenate(mats, axis=0).astype(np.float32)
    return np.concatenate([tab] * 3, axis=1), levels


def _hgrn_body(q_ref, f_ref, i_ref, g_ref, lb_ref, ng_ref, s0_ref, tab_ref,
               o_ref, sfin_ref, st_ref, *, chunk, valid, levels, n_chunks, heads):
    c = pl.program_id(2)
    hs = range(heads)
    cols = [slice(i * HEAD, (i + 1) * HEAD) for i in hs]

    @pl.when(c == 0)
    def _():
        for i in hs:
            st_ref[i] = s0_ref[i].T

    def pad(a):
        if valid == chunk:
            return a
        return jnp.concatenate([a, jnp.zeros((chunk - valid, HEAD), F32)], axis=0)

    lf, k, q, vb = [], [], [], []
    for i in hs:
        lb = lb_ref[:, cols[i]]
        f = lb + (1.0 - lb) * jax.nn.sigmoid(f_ref[:, cols[i]])
        lf.append(pad(jnp.log(f)))
        k.append(pad(1.0 - f))
        q.append(pad(q_ref[:, cols[i]]))
        vb.append(pad(i_ref[:, cols[i]]).astype(BF16))

    pieces = []
    for i in hs:
        hi = lf[i].astype(BF16)
        r1 = lf[i] - hi.astype(F32)
        mid = r1.astype(BF16)
        lo = (r1 - mid.astype(F32)).astype(BF16)
        pieces.append(jnp.concatenate([hi, mid, lo], axis=0))
    e_all = _dot(tab_ref[...], jnp.concatenate(pieces, axis=1))

    def table(t, i):
        return e_all[t * chunk:(t + 1) * chunk, cols[i]]

    row = lax.broadcasted_iota(jnp.int32, (chunk, chunk), 0)
    col = lax.broadcasted_iota(jnp.int32, (chunk, chunk), 1)
    trow = lax.broadcasted_iota(jnp.int32, (chunk, 1), 0)

    attn = [jnp.where(row == col, _dot_nt(q[i].astype(BF16), k[i].astype(BF16)), 0.0) for i in hs]
    for lv in range(levels):
        odd = ((trow >> lv) & 1) == 1
        pair = ((row >> (lv + 1)) == (col >> (lv + 1))) & (((row >> lv) & 1) == 1) \
            & (((col >> lv) & 1) == 0)
        for i in hs:
            ex = jnp.exp(table(2 + lv, i))
            qs = jnp.where(odd, q[i] * ex, 0.0).astype(BF16)
            ks = jnp.where(odd, 0.0, k[i] * ex).astype(BF16)
            attn[i] = attn[i] + jnp.where(pair, _dot_nt(qs, ks), 0.0)

    for i in hs:
        b = table(0, i)
        st = st_ref[i]
        o = _dot_nt((q[i] * jnp.exp(b)).astype(BF16), st.astype(BF16)) \
            + _dot(attn[i].astype(BF16), vb[i])
        b_last = b[chunk - 1:chunk]
        st_new = jnp.exp(b_last) * st + _dot_tn(vb[i], (k[i] * jnp.exp(table(1, i))).astype(BF16))
        st_ref[i] = st_new

        o = o[0:valid]
        o = o * lax.rsqrt(jnp.mean(o * o, axis=-1, keepdims=True) + RMS_EPS) * ng_ref[:, cols[i]]
        o_ref[:, cols[i]] = (o * _silu(g_ref[:, cols[i]])).astype(o_ref.dtype)

    @pl.when(c == n_chunks - 1)
    def _():
        for i in hs:
            sfin_ref[i] = st_ref[i].T


def _hgrn(proj, lb, norm_g, s0, *, batch, seq, out_dtype):
    if seq >= HEAD:
        chunk = valid = HEAD
        heads = HGRN_HEADS_PER_STEP
    else:
        chunk, valid = HGRN_PAD_CHUNK, seq
        heads = N_HEADS
    n_chunks = seq // valid
    groups = N_HEADS // heads
    tables, levels = _hgrn_tables(chunk)
    tables = jnp.asarray(tables, BF16)
    width = heads * HEAD

    def col_spec(part):
        return pl.BlockSpec((valid, width), lambda b, h, c: (b * n_chunks + c, part * groups + h))

    head_vec = pl.BlockSpec((1, width), lambda b, h, c: (0, h))
    state_spec = pl.BlockSpec((None, heads, HEAD, HEAD), lambda b, h, c: (b, h, 0, 0))
    o, s_fin = pl.pallas_call(
        functools.partial(_hgrn_body, chunk=chunk, valid=valid, levels=levels,
                          n_chunks=n_chunks, heads=heads),
        grid=(batch, groups, n_chunks),
        in_specs=[col_spec(0), col_spec(1), col_spec(2), col_spec(3), head_vec, head_vec,
                  state_spec, pl.BlockSpec(tables.shape, lambda b, h, c: (0, 0))],
        out_specs=[pl.BlockSpec((valid, width), lambda b, h, c: (b * n_chunks + c, h)), state_spec],
        out_shape=[jax.ShapeDtypeStruct((batch * seq, D_MODEL), out_dtype),
                   jax.ShapeDtypeStruct((batch, N_HEADS, HEAD, HEAD), F32)],
        scratch_shapes=[pltpu.VMEM((heads, HEAD, HEAD), F32)],
        compiler_params=_params(),
        name="hgrn",
    )(proj, proj, proj, proj, lb.reshape(1, D_MODEL), norm_g.reshape(1, D_MODEL), s0, tables)
    return o, s_fin


def _select_bias(gate, n_valid_blocks, axis):
    nb = gate.shape[axis]
    blk = lax.broadcasted_iota(jnp.int32, gate.shape, axis)
    valid = blk < n_valid_blocks
    beaten = jnp.zeros(gate.shape, jnp.int32)
    for m in range(min(nb, n_valid_blocks)):
        gm = gate[m:m + 1, :] if axis == 0 else gate[:, m:m + 1]
        wins = (gm > gate) | ((gm == gate) & (m < blk))
        beaten = beaten + jnp.where(wins, 1, 0)
    return jnp.where(valid & (beaten < MOBA_TOPK), 0.0, NEG_INF)


def _moba_seq_body(q_ref, k_ref, v_ref, o_ref, kb_ref, vt_ref, kmean_ref, *, nb):
    i = pl.program_id(2)

    @pl.when(i == 0)
    def _():
        kf = k_ref[...]
        kb_ref[...] = kf.astype(BF16)
        vt_ref[...] = v_ref[...].T.astype(BF16)
        means = [jnp.mean(kf[n * MOBA_BLOCK:(n + 1) * MOBA_BLOCK], axis=0, keepdims=True)
                 for n in range(nb)]
        kmean_ref[...] = jnp.concatenate(means, axis=0)

    qt = q_ref[...].T
    qtb = qt.astype(BF16)
    gate = _dot(kmean_ref[...], qt, precision=lax.Precision.HIGHEST)
    key = lax.broadcasted_iota(jnp.int32, (MOBA_BLOCK, MOBA_BLOCK), 0)
    qry = lax.broadcasted_iota(jnp.int32, (MOBA_BLOCK, MOBA_BLOCK), 1)

    def attend(own):
        n_keys = (own + 1) * MOBA_BLOCK
        bias = _select_bias(gate, own, axis=0)
        s_all = _dot(kb_ref[0:n_keys, :], qtb)
        s = [s_all[n * MOBA_BLOCK:(n + 1) * MOBA_BLOCK] + bias[n:n + 1] for n in range(own)]
        s.append(jnp.where(key <= qry, s_all[own * MOBA_BLOCK:n_keys], NEG_INF))
        m = jnp.max(s[own], axis=0, keepdims=True)
        for n in range(own):
            m = jnp.maximum(m, jnp.max(s[n], axis=0, keepdims=True))
        l = jnp.zeros_like(m)
        ps = []
        for n in range(own + 1):
            p = jnp.exp(s[n] - m)
            l = l + jnp.sum(p, axis=0, keepdims=True)
            ps.append(p.astype(BF16))
        acc = _dot(vt_ref[:, 0:n_keys], jnp.concatenate(ps, axis=0))
        o_ref[...] = (acc / l).T.astype(o_ref.dtype)

    for own in range(nb):
        pl.when(i == own)(functools.partial(attend, own))


def _moba_seq(q, k, v, *, batch, seq):
    nb = seq // MOBA_BLOCK
    return pl.pallas_call(
        functools.partial(_moba_seq_body, nb=nb),
        grid=(batch, N_HEADS, nb),
        in_specs=[pl.BlockSpec((MOBA_BLOCK, HEAD), lambda b, h, i: (b * nb + i, h)),
                  pl.BlockSpec((seq, HEAD), lambda b, h, i: (b, h)),
                  pl.BlockSpec((seq, HEAD), lambda b, h, i: (b, h))],
        out_specs=pl.BlockSpec((MOBA_BLOCK, HEAD), lambda b, h, i: (b * nb + i, h)),
        out_shape=jax.ShapeDtypeStruct((batch * seq, D_MODEL), BF16),
        scratch_shapes=[pltpu.VMEM((seq, HEAD), BF16), pltpu.VMEM((HEAD, seq), BF16),
                        pltpu.VMEM((nb, HEAD), F32)],
        compiler_params=_params(),
        name="moba_prompt",
    )(q, k, v)


def _moba_paged_body(pt_ref, q_ref, kn_ref, vn_ref, *refs, nb, tq):
    del pt_ref
    n_page_refs = PAGED_BLOCKS_PER_STEP * PAGES_PER_BLOCK
    k_refs, v_refs = refs[:n_page_refs], refs[n_page_refs:2 * n_page_refs]
    o_ref, qs_ref, hmask_ref, kmean_ref, m_ref, l_ref, part_ref = refs[2 * n_page_refs:]
    step = pl.program_id(1)
    n_steps = nb // PAGED_BLOCKS_PER_STEP
    rows = N_HEADS * tq
    page_rows = PAGE_SIZE * N_HEADS
    head_shift = N_HEADS.bit_length() - 1
    tq_shift = tq.bit_length() - 1

    @pl.when(step == 0)
    def _():
        q = q_ref[...]
        qs_ref[...] = jnp.concatenate([q[:, h * HEAD:(h + 1) * HEAD] for h in range(N_HEADS)],
                                      axis=0)
        r = lax.broadcasted_iota(jnp.int32, (rows, page_rows), 0)
        cc = lax.broadcasted_iota(jnp.int32, (rows, page_rows), 1)
        hmask_ref[...] = jnp.where((cc & (N_HEADS - 1)) == (r >> tq_shift), 0.0, NEG_INF)
        m_ref[...] = jnp.zeros_like(m_ref)
        l_ref[...] = jnp.zeros_like(l_ref)

    qsb = qs_ref[...].astype(BF16)
    hmask = hmask_ref[...]
    lane = lax.broadcasted_iota(jnp.int32, (rows, LANES), 1)

    for bi in range(PAGED_BLOCKS_PER_STEP):
        n = step * PAGED_BLOCKS_PER_STEP + bi
        k_pages = k_refs[bi * PAGES_PER_BLOCK:(bi + 1) * PAGES_PER_BLOCK]
        v_pages = v_refs[bi * PAGES_PER_BLOCK:(bi + 1) * PAGES_PER_BLOCK]

        ksum = jnp.sum(k_pages[0][...], axis=0)
        for kp in k_pages[1:]:
            ksum = ksum + jnp.sum(kp[...], axis=0)
        kmean_ref[n] = ksum * (1.0 / MOBA_BLOCK)

        s = [_dot_nt(qsb, kp[...].reshape(page_rows, HEAD).astype(BF16)) + hmask for kp in k_pages]
        m_n = jnp.max(s[0], axis=-1, keepdims=True)
        for sp in s[1:]:
            m_n = jnp.maximum(m_n, jnp.max(sp, axis=-1, keepdims=True))
        l_n = jnp.zeros_like(m_n)
        o_n = jnp.zeros((rows, HEAD), F32)
        for sp, vp in zip(s, v_pages):
            p = jnp.exp(sp - m_n)
            l_n = l_n + jnp.sum(p, axis=-1, keepdims=True)
            o_n = o_n + _dot(p.astype(BF16), vp[...].reshape(page_rows, HEAD).astype(BF16))
        part_ref[n] = o_n
        m_ref[...] = jnp.where(lane == n, m_n, m_ref[...])
        l_ref[...] = jnp.where(lane == n, l_n, l_ref[...])

    @pl.when(step == n_steps - 1)
    def _():
        qs = qs_ref[...]
        gates = [_dot_nt(qs[h * tq:(h + 1) * tq], kmean_ref[:, h, :], precision=lax.Precision.HIGHEST)
                 for h in range(N_HEADS)]
        bias = _select_bias(jnp.concatenate(gates, axis=0), nb, axis=1)

        s_own = _dot_nt(qsb, kn_ref[...].astype(BF16))
        r = lax.broadcasted_iota(jnp.int32, s_own.shape, 0)
        cc = lax.broadcasted_iota(jnp.int32, s_own.shape, 1)
        keep = ((cc & (N_HEADS - 1)) == (r >> tq_shift)) & ((cc >> head_shift) <= (r & (tq - 1)))
        s_own = jnp.where(keep, s_own, NEG_INF)
        m_own = jnp.max(s_own, axis=-1, keepdims=True)
        p_own = jnp.exp(s_own - m_own)
        l_own = jnp.sum(p_own, axis=-1, keepdims=True)
        o_own = _dot(p_own.astype(BF16), vn_ref[...].astype(BF16))

        mb = m_ref[:, 0:nb] + bias
        m = jnp.maximum(m_own, jnp.max(mb, axis=-1, keepdims=True))
        w = jnp.exp(mb - m)
        w_own = jnp.exp(m_own - m)
        l = w_own * l_own + jnp.sum(w * l_ref[:, 0:nb], axis=-1, keepdims=True)
        o = w_own * o_own
        for j in range(nb):
            o = o + w[:, j:j + 1] * part_ref[j]
        o = o / l
        for h in range(N_HEADS):
            o_ref[:, h * HEAD:(h + 1) * HEAD] = o[h * tq:(h + 1) * tq]


def _moba_paged(q, k_new, v_new, cache_k, cache_v, page_table, *, batch, tq):
    n_pages = page_table.shape[1]
    nb = n_pages // PAGES_PER_BLOCK
    rows = N_HEADS * tq
    assert nb <= LANES and tq & (tq - 1) == 0 and rows % 8 == 0
    assert nb % PAGED_BLOCKS_PER_STEP == 0
    page_block = (None, PAGE_SIZE, N_HEADS, HEAD)
    pages_per_step = PAGED_BLOCKS_PER_STEP * PAGES_PER_BLOCK

    def page(p):
        return pl.BlockSpec(page_block, lambda b, n, pt: (pt[b, pages_per_step * n + p], 0, 0, 0))

    pages = [page(p) for p in range(pages_per_step)]
    q_spec = pl.BlockSpec((tq, D_MODEL), lambda b, n, pt: (b, 0))
    new_spec = pl.BlockSpec((rows, HEAD), lambda b, n, pt: (b, 0))
    grid_spec = pltpu.PrefetchScalarGridSpec(
        num_scalar_prefetch=1,
        grid=(batch, nb // PAGED_BLOCKS_PER_STEP),
        in_specs=[q_spec, new_spec, new_spec] + pages + pages,
        out_specs=q_spec,
        scratch_shapes=[pltpu.VMEM((rows, HEAD), F32),
                        pltpu.VMEM((rows, PAGE_SIZE * N_HEADS), F32),
                        pltpu.VMEM((nb, N_HEADS, HEAD), F32),
                        pltpu.VMEM((rows, LANES), F32),
                        pltpu.VMEM((rows, LANES), F32),
                        pltpu.VMEM((nb, rows, HEAD), F32)],
    )
    k_rows = k_new.reshape(batch * rows, HEAD)
    v_rows = v_new.reshape(batch * rows, HEAD)
    return pl.pallas_call(
        functools.partial(_moba_paged_body, nb=nb, tq=tq),
        grid_spec=grid_spec,
        out_shape=jax.ShapeDtypeStruct((batch * tq, D_MODEL), F32),
        compiler_params=_params(),
        name="moba_paged",
    )(page_table, q, k_rows, v_rows, *([cache_k] * pages_per_step), *([cache_v] * pages_per_step))


def _tile_sizes(m):
    if m >= 1024:
        return dict(tm_wide=1024, tm_ln=512)
    return dict(tm_wide=m, tm_ln=m)


def _trunk(x, mod, mod_kv, s0, attend, weights, *, batch, seq):
    (a_w_in, lb_all, a_norm_g, a_w_out, b_w_q, b_w_o, w_kv, ln_g, ln_b, ffn_w_in, ffn_w_out) = weights
    m = batch * seq
    x = x.reshape(m, D_MODEL)
    ts = _tile_sizes(m)
    tm_wide, tm_ln = ts["tm_wide"], ts["tm_ln"]
    per_row = seq < tm_ln
    rows_per_group = m if per_row else seq

    def vecs(mvec, n):
        out = []
        for part in jnp.split(mvec, n, axis=-1):
            if per_row:
                out.append(jnp.repeat(part, seq, axis=0).reshape(1, m, D_MODEL))
            else:
                out.append(part.reshape(batch, 1, D_MODEL))
        return out

    sh1, sc1, g1, sh2, sc2, g2 = vecs(mod[0], 6)
    sh1b, sc1b, g1b, sh2b, sc2b, g2b = vecs(mod[1], 6)
    shift_kv, scale_kv = vecs(mod_kv, 2)
    common = dict(tm=tm_ln, tn=512, rows_per_group=rows_per_group)
    narrow = dict(tm_mm=tm_wide, **common)
    deep = dict(tm_mm=tm_ln, **common)

    h = _modulate(x, sc1, sh1, tm=tm_ln, rows_per_group=rows_per_group)
    proj = _matmul(h, a_w_in, tm=tm_wide, tn=512, name="hgrn_in_proj")
    o, s_fin = _hgrn(proj, lb_all[0], a_norm_g[0], s0[0], batch=batch, seq=seq,
                     out_dtype=BF16 if seq >= HEAD else F32)
    x, (h,) = _matmul_ln(o, a_w_out, x, g1, ln_g[0, 0], ln_b[0, 0], [(sc2, sh2)],
                         name="hgrn_out_proj", **narrow)
    hid = _matmul_pair(h, ffn_w_in, layer=0, swiglu=True, tm=tm_wide, tn=512, name="swiglu_in")
    x, (h, z) = _matmul_ln(hid, ffn_w_out, x, g2, ln_g[0, 1], ln_b[0, 1],
                           [(sc1b, sh1b), (scale_kv, shift_kv)], layer=0,
                           name="ffn0_out_proj", **deep)

    k, v = _matmul_pair(z, w_kv, swiglu=False, tm=tm_wide, tn=512, name="kv_proj")
    q = _matmul(h, b_w_q, scale=HEAD ** -0.5, tm=tm_wide, tn=512, name="q_proj")
    o = attend(q, k, v)
    x, (h,) = _matmul_ln(o, b_w_o, x, g1b, ln_g[1, 0], ln_b[1, 0], [(sc2b, sh2b)],
                         name="moba_out_proj", **narrow)
    hid = _matmul_pair(h, ffn_w_in, layer=1, swiglu=True, tm=tm_wide, tn=512, name="swiglu_in")
    y, _ = _matmul_ln(hid, ffn_w_out, x, g2b, ln_g[1, 1], ln_b[1, 1], [], layer=1,
                      name="ffn1_out_proj", **deep)

    kv_shape = (batch, seq, N_HEADS, HEAD)
    return y.reshape(batch, seq, D_MODEL), s_fin[None], k.reshape(kv_shape), v.reshape(kv_shape)


def kernel(x_prompt, x_sample, state_hgrn, cache_k, cache_v, page_table, c_prompt, c_sample,
           a_w_in, a_lb_logits, a_norm_g, a_w_out, b_w_q, b_w_o, w_ada_kv, b_ada_kv, w_kv,
           w_ada, b_ada, ln_g, ln_b, ffn_w_in, ffn_w_out):
    assert a_w_in.shape[0] == 1 and b_w_q.shape[0] == 1 and w_ada.shape[0] == DEPTH
    bp, tp, _ = x_prompt.shape
    bs, tsamp, _ = x_sample.shape

    lb_all = _lower_bounds(a_lb_logits.astype(F32))

    c_all = jnp.concatenate([c_prompt, c_sample], axis=0)
    n_c = c_all.shape[0]
    c_rows = -(-n_c // BF16_ROWS) * BF16_ROWS
    c_all = jnp.pad(c_all, ((0, c_rows - n_c), (0, 0)))
    ada = functools.partial(_matmul, silu_in=True, tm=c_rows, tn=1024)
    mod = [ada(c_all, w_ada, layer=l, bias=b_ada[l], name="ada_layer") for l in range(DEPTH)]
    mod_kv = ada(c_all, w_ada_kv, bias=b_ada_kv, name="ada_kv")

    weights = (a_w_in, lb_all, a_norm_g, a_w_out, b_w_q, b_w_o, w_kv, ln_g, ln_b, ffn_w_in, ffn_w_out)

    s0_prompt = jnp.zeros((a_w_in.shape[0], bp, N_HEADS, HEAD, HEAD), state_hgrn.dtype)
    y_p, s_p, k_p, v_p = _trunk(
        x_prompt, [mm[:bp] for mm in mod], mod_kv[:bp], s0_prompt,
        functools.partial(_moba_seq, batch=bp, seq=tp), weights, batch=bp, seq=tp)
    y_s, s_s, k_s, v_s = _trunk(
        x_sample, [mm[bp:bp + bs] for mm in mod], mod_kv[bp:bp + bs], state_hgrn,
        lambda q, k, v: _moba_paged(q, k, v, cache_k, cache_v, page_table, batch=bs, tq=tsamp),
        weights, batch=bs, seq=tsamp)
    return (y_p, y_s, s_p, s_s, k_p, v_p, k_s, v_s)
```

```python
import functools

import numpy as np
import jax
import jax.numpy as jnp
from jax import lax
from jax.experimental import pallas as pl
from jax.experimental.pallas import tpu as pltpu

F32 = jnp.float32
BF16 = jnp.bfloat16

D_MODEL = 2048
DEPTH = 2
HEAD = 128
N_HEADS = D_MODEL // HEAD
HGRN_PAD_CHUNK = 16
HGRN_HEADS_PER_STEP = 8
MOBA_BLOCK = 256
MOBA_TOPK = 3
MOBA_HEADS_PER_STEP = 2
PAGE_SIZE = 128
PAGES_PER_BLOCK = MOBA_BLOCK // PAGE_SIZE
PAGED_BLOCKS_PER_STEP = 2
BF16_ROWS = 16
LANES = 128
LN_EPS = 1e-5
RMS_EPS = 1e-6
ALPHA = (2.0 * DEPTH) ** 0.25
VMEM_LIMIT_BYTES = 56 * 1024 * 1024

NEG_INF = float("-inf")


def _params():
    return pltpu.CompilerParams(vmem_limit_bytes=VMEM_LIMIT_BYTES)


def _dot(a, b, precision=None):
    return lax.dot_general(a, b, (((1,), (0,)), ((), ())), preferred_element_type=F32,
                           precision=precision)


def _dot_nt(a, b, precision=None):
    return lax.dot_general(a, b, (((1,), (1,)), ((), ())), preferred_element_type=F32,
                           precision=precision)


def _dot_tn(a, b):
    return lax.dot_general(a, b, (((0,), (0,)), ((), ())), preferred_element_type=F32)


def _silu(x):
    return x * jax.nn.sigmoid(x)


def _weight_spec(w, layer, rows, cols, index_map):
    if w.ndim == 2:
        return pl.BlockSpec((rows, cols), index_map)
    return pl.BlockSpec((None, rows, cols), lambda *g: (layer,) + tuple(index_map(*g)))


def _lb_body(logit_ref, out_ref):
    z = logit_ref[...]
    n = z.shape[0]
    m = z[0:1]
    for r in range(1, n):
        m = jnp.maximum(m, z[r:r + 1])
    e = jnp.exp(z - m)
    tot = e[0:1]
    for r in range(1, n):
        tot = tot + e[r:r + 1]
    p = e / tot
    run = p[0:1]
    out_ref[0:1, :] = run
    for r in range(1, n):
        run = run + p[r:r + 1]
        out_ref[r:r + 1, :] = run


def _lower_bounds(logits):
    return pl.pallas_call(
        _lb_body, out_shape=jax.ShapeDtypeStruct(logits.shape, F32), name="lower_bounds")(logits)


def _mm_body(*refs, silu_in, scale, has_bias):
    if has_bias:
        x_ref, w_ref, b_ref, o_ref, wb_ref = refs
    else:
        x_ref, w_ref, o_ref, wb_ref = refs

    @pl.when(pl.program_id(1) == 0)
    def _():
        wb_ref[...] = w_ref[...].astype(BF16)

    x = x_ref[...]
    if silu_in:
        x = _silu(x)
    acc = _dot(x.astype(BF16), wb_ref[...])
    if has_bias:
        acc = acc + b_ref[...]
    if scale is not None:
        acc = acc * scale
    o_ref[...] = acc.astype(o_ref.dtype)


def _matmul(x, w, *, layer=0, bias=None, silu_in=False, scale=None, tm, tn, name):
    m, k = x.shape
    n = w.shape[-1]
    in_specs = [pl.BlockSpec((tm, k), lambda j, i: (i, 0)),
                _weight_spec(w, layer, k, tn, lambda j, i: (0, j))]
    args = [x, w]
    if bias is not None:
        in_specs.append(pl.BlockSpec((1, tn), lambda j, i: (0, j)))
        args.append(bias.reshape(1, n))
    return pl.pallas_call(
        functools.partial(_mm_body, silu_in=silu_in, scale=scale, has_bias=bias is not None),
        grid=(n // tn, m // tm),
        in_specs=in_specs,
        out_specs=pl.BlockSpec((tm, tn), lambda j, i: (i, j)),
        out_shape=jax.ShapeDtypeStruct((m, n), F32),
        scratch_shapes=[pltpu.VMEM((k, tn), BF16)],
        compiler_params=_params(),
        name=name,
    )(*args)


def _pair_body(*refs, swiglu):
    if swiglu:
        x_ref, wa_ref, wu_ref, o_ref, wab_ref, wub_ref = refs
    else:
        x_ref, wa_ref, wu_ref, oa_ref, ou_ref, wab_ref, wub_ref = refs

    @pl.when(pl.program_id(1) == 0)
    def _():
        wab_ref[...] = wa_ref[...].astype(BF16)
        wub_ref[...] = wu_ref[...].astype(BF16)

    x = x_ref[...]
    a = _dot(x, wab_ref[...])
    u = _dot(x, wub_ref[...])
    if swiglu:
        o_ref[...] = (_silu(a) * u).astype(o_ref.dtype)
    else:
        oa_ref[...] = a
        ou_ref[...] = u


def _matmul_pair(x, w, *, layer=0, swiglu, tm, tn, name):
    m, k = x.shape
    f = w.shape[-1] // 2
    nj = f // tn
    out_spec = pl.BlockSpec((tm, tn), lambda j, i: (i, j))
    if swiglu:
        out_specs, out_shape = out_spec, jax.ShapeDtypeStruct((m, f), BF16)
    else:
        out_specs, out_shape = [out_spec] * 2, [jax.ShapeDtypeStruct((m, f), F32)] * 2
    return pl.pallas_call(
        functools.partial(_pair_body, swiglu=swiglu),
        grid=(nj, m // tm),
        in_specs=[pl.BlockSpec((tm, k), lambda j, i: (i, 0)),
                  _weight_spec(w, layer, k, tn, lambda j, i: (0, j)),
                  _weight_spec(w, layer, k, tn, lambda j, i: (0, j + nj))],
        out_specs=out_specs,
        out_shape=out_shape,
        scratch_shapes=[pltpu.VMEM((k, tn), BF16), pltpu.VMEM((k, tn), BF16)],
        compiler_params=_params(),
        name=name,
    )(x, w, w)


def _modulate_body(x_ref, sc_ref, sh_ref, o_ref):
    o_ref[...] = (x_ref[...] * (1.0 + sc_ref[...]) + sh_ref[...]).astype(o_ref.dtype)


def _mod_spec(vec, tm, rows_per_group):
    r = vec.shape[1]
    tiles = rows_per_group // tm
    return pl.BlockSpec((None, r, D_MODEL), lambda i, *_: (i // tiles, 0, 0))


def _modulate(x, sc, sh, *, tm, rows_per_group):
    m = x.shape[0]
    return pl.pallas_call(
        _modulate_body,
        grid=(m // tm,),
        in_specs=[pl.BlockSpec((tm, D_MODEL), lambda i: (i, 0)),
                  _mod_spec(sc, tm, rows_per_group), _mod_spec(sh, tm, rows_per_group)],
        out_specs=pl.BlockSpec((tm, D_MODEL), lambda i: (i, 0)),
        out_shape=jax.ShapeDtypeStruct((m, D_MODEL), BF16),
        compiler_params=_params(),
        name="modulate",
    )(x, sc, sh)


def _ln_body(*refs, n_mod):
    mix_ref, x_ref, gate_ref, lng_ref, lnb_ref = refs[:5]
    mod_refs = refs[5:5 + 2 * n_mod]
    y_ref = refs[5 + 2 * n_mod]
    h_refs = refs[6 + 2 * n_mod:]
    r = ALPHA * x_ref[...] + (1.0 + gate_ref[...]) * mix_ref[...]
    mu = jnp.mean(r, axis=-1, keepdims=True)
    d = r - mu
    var = jnp.mean(d * d, axis=-1, keepdims=True)
    y = d * lax.rsqrt(var + LN_EPS) * lng_ref[...] + lnb_ref[...]
    y_ref[...] = y
    for j in range(n_mod):
        sc_ref, sh_ref = mod_refs[2 * j], mod_refs[2 * j + 1]
        h_refs[j][...] = (y * (1.0 + sc_ref[...]) + sh_ref[...]).astype(BF16)


def _matmul_ln(a, w, x, gate, ln_g, ln_b, mods, *, layer=0, tm_mm, tn, tm, rows_per_group, name):
    mix = _matmul(a, w, layer=layer, tm=tm_mm, tn=tn, name=name)
    m = x.shape[0]
    n_mod = len(mods)
    row_spec = pl.BlockSpec((tm, D_MODEL), lambda i: (i, 0))
    vec_spec = pl.BlockSpec((1, D_MODEL), lambda i: (0, 0))
    in_specs = [row_spec, row_spec, _mod_spec(gate, tm, rows_per_group), vec_spec, vec_spec]
    args = [mix, x, gate, ln_g.reshape(1, D_MODEL), ln_b.reshape(1, D_MODEL)]
    for sc, sh in mods:
        in_specs += [_mod_spec(sc, tm, rows_per_group), _mod_spec(sh, tm, rows_per_group)]
        args += [sc, sh]
    out = pl.pallas_call(
        functools.partial(_ln_body, n_mod=n_mod),
        grid=(m // tm,),
        in_specs=in_specs,
        out_specs=[row_spec] * (1 + n_mod),
        out_shape=[jax.ShapeDtypeStruct((m, D_MODEL), F32)]
        + [jax.ShapeDtypeStruct((m, D_MODEL), BF16)] * n_mod,
        compiler_params=_params(),
        name="residual_ln",
    )(*args)
    return out[0], list(out[1:])


def _hgrn_tables(chunk):
    levels = chunk.bit_length() - 1
    t = np.arange(chunk)[:, None]
    r = np.arange(chunk)[None, :]
    mats = [(r <= t), (r > t)]
    for lv in range(levels):
        size = 1 << lv
        odd = ((t >> lv) & 1) == 1
        start = t & ~(size - 1)
        end = t | (size - 1)
        mats.append(np.where(odd, (r >= start) & (r <= t), (r > t) & (r <= end)))
    tab = np.concatenate(mats, axis=0).astype(np.float32)
    return np.concatenate([tab] * 3, axis=1), levels


def _hgrn_body(q_ref, f_ref, i_ref, g_ref, lb_ref, ng_ref, s0_ref, tab_ref,
               o_ref, sfin_ref, st_ref, *, chunk, valid, levels, n_chunks, heads):
    c = pl.program_id(2)
    hs = range(heads)
    cols = [slice(i * HEAD, (i + 1) * HEAD) for i in hs]

    @pl.when(c == 0)
    def _():
        for i in hs:
            st_ref[i] = s0_ref[i].T

    def pad(a):
        if valid == chunk:
            return a
        return jnp.concatenate([a, jnp.zeros((chunk - valid, HEAD), F32)], axis=0)

    lf, k, q, vb = [], [], [], []
    for i in hs:
        lb = lb_ref[:, cols[i]]
        f = lb + (1.0 - lb) * jax.nn.sigmoid(f_ref[:, cols[i]])
        lf.append(pad(jnp.log(f)))
        k.append(pad(1.0 - f))
        q.append(pad(q_ref[:, cols[i]]))
        vb.append(pad(i_ref[:, cols[i]]).astype(BF16))

    pieces = []
    for i in hs:
        hi = lf[i].astype(BF16)
        r1 = lf[i] - hi.astype(F32)
        mid = r1.astype(BF16)
        lo = (r1 - mid.astype(F32)).astype(BF16)
        pieces.append(jnp.concatenate([hi, mid, lo], axis=0))
    e_all = _dot(tab_ref[...], jnp.concatenate(pieces, axis=1))

    def table(t, i):
        return e_all[t * chunk:(t + 1) * chunk, cols[i]]

    row = lax.broadcasted_iota(jnp.int32, (chunk, chunk), 0)
    col = lax.broadcasted_iota(jnp.int32, (chunk, chunk), 1)
    trow = lax.broadcasted_iota(jnp.int32, (chunk, 1), 0)

    attn = [jnp.where(row == col, _dot_nt(q[i].astype(BF16), k[i].astype(BF16)), 0.0) for i in hs]
    for lv in range(levels):
        odd = ((trow >> lv) & 1) == 1
        pair = ((row >> (lv + 1)) == (col >> (lv + 1))) & (((row >> lv) & 1) == 1) \
            & (((col >> lv) & 1) == 0)
        for i in hs:
            ex = jnp.exp(table(2 + lv, i))
            qs = jnp.where(odd, q[i] * ex, 0.0).astype(BF16)
            ks = jnp.where(odd, 0.0, k[i] * ex).astype(BF16)
            attn[i] = attn[i] + jnp.where(pair, _dot_nt(qs, ks), 0.0)

    for i in hs:
        b = table(0, i)
        st = st_ref[i]
        o = _dot_nt((q[i] * jnp.exp(b)).astype(BF16), st.astype(BF16)) \
            + _dot(attn[i].astype(BF16), vb[i])
        b_last = b[chunk - 1:chunk]
        st_new = jnp.exp(b_last) * st + _dot_tn(vb[i], (k[i] * jnp.exp(table(1, i))).astype(BF16))
        st_ref[i] = st_new

        o = o[0:valid]
        o = o * lax.rsqrt(jnp.mean(o * o, axis=-1, keepdims=True) + RMS_EPS) * ng_ref[:, cols[i]]
        o_ref[:, cols[i]] = (o * _silu(g_ref[:, cols[i]])).astype(o_ref.dtype)

    @pl.when(c == n_chunks - 1)
    def _():
        for i in hs:
            sfin_ref[i] = st_ref[i].T


def _hgrn(proj, lb, norm_g, s0, *, batch, seq, out_dtype):
    if seq >= HEAD:
        chunk = valid = HEAD
        heads = HGRN_HEADS_PER_STEP
    else:
        chunk, valid = HGRN_PAD_CHUNK, seq
        heads = N_HEADS
    n_chunks = seq // valid
    groups = N_HEADS // heads
    tables, levels = _hgrn_tables(chunk)
    tables = jnp.asarray(tables, BF16)
    width = heads * HEAD

    def col_spec(part):
        return pl.BlockSpec((valid, width), lambda b, h, c: (b * n_chunks + c, part * groups + h))

    head_vec = pl.BlockSpec((1, width), lambda b, h, c: (0, h))
    state_spec = pl.BlockSpec((None, heads, HEAD, HEAD), lambda b, h, c: (b, h, 0, 0))
    o, s_fin = pl.pallas_call(
        functools.partial(_hgrn_body, chunk=chunk, valid=valid, levels=levels,
                          n_chunks=n_chunks, heads=heads),
        grid=(batch, groups, n_chunks),
        in_specs=[col_spec(0), col_spec(1), col_spec(2), col_spec(3), head_vec, head_vec,
                  state_spec, pl.BlockSpec(tables.shape, lambda b, h, c: (0, 0))],
        out_specs=[pl.BlockSpec((valid, width), lambda b, h, c: (b * n_chunks + c, h)), state_spec],
        out_shape=[jax.ShapeDtypeStruct((batch * seq, D_MODEL), out_dtype),
                   jax.ShapeDtypeStruct((batch, N_HEADS, HEAD, HEAD), F32)],
        scratch_shapes=[pltpu.VMEM((heads, HEAD, HEAD), F32)],
        compiler_params=_params(),
        name="hgrn",
    )(proj, proj, proj, proj, lb.reshape(1, D_MODEL), norm_g.reshape(1, D_MODEL), s0, tables)
    return o, s_fin


def _select_bias(gate, n_valid_blocks, axis):
    nb = gate.shape[axis]
    blk = lax.broadcasted_iota(jnp.int32, gate.shape, axis)
    valid = blk < n_valid_blocks
    beaten = jnp.zeros(gate.shape, jnp.int32)
    for m in range(min(nb, n_valid_blocks)):
        gm = gate[m:m + 1, :] if axis == 0 else gate[:, m:m + 1]
        wins = (gm > gate) | ((gm == gate) & (m < blk))
        beaten = beaten + jnp.where(wins, 1, 0)
    return jnp.where(valid & (beaten < MOBA_TOPK), 0.0, NEG_INF)


def _moba_seq_body(q_ref, k_ref, v_ref, o_ref, kb_ref, vt_ref, kmean_ref, *, nb, heads):
    i = pl.program_id(2)
    hs = range(heads)
    cols = [slice(h * HEAD, (h + 1) * HEAD) for h in hs]

    @pl.when(i == 0)
    def _():
        for h in hs:
            kf = k_ref[:, cols[h]]
            kb_ref[h] = kf.astype(BF16)
            vt_ref[h] = v_ref[:, cols[h]].T.astype(BF16)
            means = [jnp.mean(kf[n * MOBA_BLOCK:(n + 1) * MOBA_BLOCK], axis=0, keepdims=True)
                     for n in range(nb)]
            kmean_ref[h] = jnp.concatenate(means, axis=0)

    qt = [q_ref[:, cols[h]].T for h in hs]
    qtb = [a.astype(BF16) for a in qt]
    gate = [_dot(kmean_ref[h], qt[h], precision=lax.Precision.HIGHEST) for h in hs]
    key = lax.broadcasted_iota(jnp.int32, (MOBA_BLOCK, MOBA_BLOCK), 0)
    qry = lax.broadcasted_iota(jnp.int32, (MOBA_BLOCK, MOBA_BLOCK), 1)

    def attend(own):
        n_keys = (own + 1) * MOBA_BLOCK
        bias = [_select_bias(gate[h], own, axis=0) for h in hs]
        s_all = [_dot(kb_ref[h, 0:n_keys, :], qtb[h]) for h in hs]
        s, m = [], []
        for h in hs:
            sh = [s_all[h][n * MOBA_BLOCK:(n + 1) * MOBA_BLOCK] + bias[h][n:n + 1]
                  for n in range(own)]
            sh.append(jnp.where(key <= qry, s_all[h][own * MOBA_BLOCK:n_keys], NEG_INF))
            mh = jnp.max(sh[own], axis=0, keepdims=True)
            for n in range(own):
                mh = jnp.maximum(mh, jnp.max(sh[n], axis=0, keepdims=True))
            s.append(sh)
            m.append(mh)
        for h in hs:
            l = jnp.zeros_like(m[h])
            ps = []
            for n in range(own + 1):
                p = jnp.exp(s[h][n] - m[h])
                l = l + jnp.sum(p, axis=0, keepdims=True)
                ps.append(p.astype(BF16))
            acc = _dot(vt_ref[h, :, 0:n_keys], jnp.concatenate(ps, axis=0))
            o_ref[:, cols[h]] = (acc / l).T.astype(o_ref.dtype)

    for own in range(nb):
        pl.when(i == own)(functools.partial(attend, own))


def _moba_seq(q, k, v, *, batch, seq):
    nb = seq // MOBA_BLOCK
    heads = MOBA_HEADS_PER_STEP
    width = heads * HEAD
    return pl.pallas_call(
        functools.partial(_moba_seq_body, nb=nb, heads=heads),
        grid=(batch, N_HEADS // heads, nb),
        in_specs=[pl.BlockSpec((MOBA_BLOCK, width), lambda b, h, i: (b * nb + i, h)),
                  pl.BlockSpec((seq, width), lambda b, h, i: (b, h)),
                  pl.BlockSpec((seq, width), lambda b, h, i: (b, h))],
        out_specs=pl.BlockSpec((MOBA_BLOCK, width), lambda b, h, i: (b * nb + i, h)),
        out_shape=jax.ShapeDtypeStruct((batch * seq, D_MODEL), BF16),
        scratch_shapes=[pltpu.VMEM((heads, seq, HEAD), BF16), pltpu.VMEM((heads, HEAD, seq), BF16),
                        pltpu.VMEM((heads, nb, HEAD), F32)],
        compiler_params=_params(),
        name="moba_prompt",
    )(q, k, v)


def _moba_paged_body(pt_ref, q_ref, kn_ref, vn_ref, *refs, nb, tq):
    del pt_ref
    n_page_refs = PAGED_BLOCKS_PER_STEP * PAGES_PER_BLOCK
    k_refs, v_refs = refs[:n_page_refs], refs[n_page_refs:2 * n_page_refs]
    o_ref, qs_ref, hmask_ref, kmean_ref, m_ref, l_ref, part_ref = refs[2 * n_page_refs:]
    step = pl.program_id(1)
    n_steps = nb // PAGED_BLOCKS_PER_STEP
    rows = N_HEADS * tq
    page_rows = PAGE_SIZE * N_HEADS
    head_shift = N_HEADS.bit_length() - 1
    tq_shift = tq.bit_length() - 1

    @pl.when(step == 0)
    def _():
        q = q_ref[...]
        qs_ref[...] = jnp.concatenate([q[:, h * HEAD:(h + 1) * HEAD] for h in range(N_HEADS)],
                                      axis=0)
        r = lax.broadcasted_iota(jnp.int32, (rows, page_rows), 0)
        cc = lax.broadcasted_iota(jnp.int32, (rows, page_rows), 1)
        hmask_ref[...] = jnp.where((cc & (N_HEADS - 1)) == (r >> tq_shift), 0.0, NEG_INF)
        m_ref[...] = jnp.zeros_like(m_ref)
        l_ref[...] = jnp.zeros_like(l_ref)

    qsb = qs_ref[...].astype(BF16)
    hmask = hmask_ref[...]
    lane = lax.broadcasted_iota(jnp.int32, (rows, LANES), 1)
    ones = jnp.ones((page_rows, HEAD), BF16)

    for bi in range(PAGED_BLOCKS_PER_STEP):
        n = step * PAGED_BLOCKS_PER_STEP + bi
        k_pages = k_refs[bi * PAGES_PER_BLOCK:(bi + 1) * PAGES_PER_BLOCK]
        v_pages = v_refs[bi * PAGES_PER_BLOCK:(bi + 1) * PAGES_PER_BLOCK]

        ksum = jnp.sum(k_pages[0][...], axis=0)
        for kp in k_pages[1:]:
            ksum = ksum + jnp.sum(kp[...], axis=0)
        kmean_ref[n] = ksum * (1.0 / MOBA_BLOCK)

        for pg, (kp, vp) in enumerate(zip(k_pages, v_pages)):
            s = _dot_nt(qsb, kp[...].reshape(page_rows, HEAD).astype(BF16)) + hmask
            m_n = jnp.max(s, axis=-1, keepdims=True)
            p = jnp.exp(s - m_n).astype(BF16)
            v_aug = jnp.concatenate([vp[...].reshape(page_rows, HEAD).astype(BF16), ones], axis=1)
            o_aug = _dot(p, v_aug)
            part_ref[pg, n] = o_aug[:, 0:HEAD]
            m_ref[pg] = jnp.where(lane == n, m_n, m_ref[pg])
            l_ref[pg] = jnp.where(lane == n, o_aug[:, HEAD:HEAD + 1], l_ref[pg])

    @pl.when(step == n_steps - 1)
    def _():
        qs = qs_ref[...]
        gates = [_dot_nt(qs[h * tq:(h + 1) * tq], kmean_ref[:, h, :], precision=lax.Precision.HIGHEST)
                 for h in range(N_HEADS)]
        bias = _select_bias(jnp.concatenate(gates, axis=0), nb, axis=1)

        s_own = _dot_nt(qsb, kn_ref[...].astype(BF16))
        r = lax.broadcasted_iota(jnp.int32, s_own.shape, 0)
        cc = lax.broadcasted_iota(jnp.int32, s_own.shape, 1)
        keep = ((cc & (N_HEADS - 1)) == (r >> tq_shift)) & ((cc >> head_shift) <= (r & (tq - 1)))
        s_own = jnp.where(keep, s_own, NEG_INF)
        m_own = jnp.max(s_own, axis=-1, keepdims=True)
        p_own = jnp.exp(s_own - m_own)
        l_own = jnp.sum(p_own, axis=-1, keepdims=True)
        o_own = _dot(p_own.astype(BF16), vn_ref[...].astype(BF16))

        mb = [m_ref[pg][:, 0:nb] + bias for pg in range(PAGES_PER_BLOCK)]
        m = m_own
        for mbp in mb:
            m = jnp.maximum(m, jnp.max(mbp, axis=-1, keepdims=True))
        w_own = jnp.exp(m_own - m)
        l = w_own * l_own
        o = w_own * o_own
        for pg in range(PAGES_PER_BLOCK):
            w = jnp.exp(mb[pg] - m)
            l = l + jnp.sum(w * l_ref[pg][:, 0:nb], axis=-1, keepdims=True)
            for j in range(nb):
                o = o + w[:, j:j + 1] * part_ref[pg, j]
        o = o / l
        for h in range(N_HEADS):
            o_ref[:, h * HEAD:(h + 1) * HEAD] = o[h * tq:(h + 1) * tq]


def _moba_paged(q, k_new, v_new, cache_k, cache_v, page_table, *, batch, tq):
    n_pages = page_table.shape[1]
    nb = n_pages // PAGES_PER_BLOCK
    rows = N_HEADS * tq
    assert nb <= LANES and tq & (tq - 1) == 0 and rows % 8 == 0
    assert nb % PAGED_BLOCKS_PER_STEP == 0
    page_block = (None, PAGE_SIZE, N_HEADS, HEAD)
    pages_per_step = PAGED_BLOCKS_PER_STEP * PAGES_PER_BLOCK

    def page(p):
        return pl.BlockSpec(page_block, lambda b, n, pt: (pt[b, pages_per_step * n + p], 0, 0, 0))

    pages = [page(p) for p in range(pages_per_step)]
    q_spec = pl.BlockSpec((tq, D_MODEL), lambda b, n, pt: (b, 0))
    new_spec = pl.BlockSpec((rows, HEAD), lambda b, n, pt: (b, 0))
    grid_spec = pltpu.PrefetchScalarGridSpec(
        num_scalar_prefetch=1,
        grid=(batch, nb // PAGED_BLOCKS_PER_STEP),
        in_specs=[q_spec, new_spec, new_spec] + pages + pages,
        out_specs=q_spec,
        scratch_shapes=[pltpu.VMEM((rows, HEAD), F32),
                        pltpu.VMEM((rows, PAGE_SIZE * N_HEADS), F32),
                        pltpu.VMEM((nb, N_HEADS, HEAD), F32),
                        pltpu.VMEM((PAGES_PER_BLOCK, rows, LANES), F32),
                        pltpu.VMEM((PAGES_PER_BLOCK, rows, LANES), F32),
                        pltpu.VMEM((PAGES_PER_BLOCK, nb, rows, HEAD), F32)],
    )
    k_rows = k_new.reshape(batch * rows, HEAD)
    v_rows = v_new.reshape(batch * rows, HEAD)
    return pl.pallas_call(
        functools.partial(_moba_paged_body, nb=nb, tq=tq),
        grid_spec=grid_spec,
        out_shape=jax.ShapeDtypeStruct((batch * tq, D_MODEL), F32),
        compiler_params=_params(),
        name="moba_paged",
    )(page_table, q, k_rows, v_rows, *([cache_k] * pages_per_step), *([cache_v] * pages_per_step))


TN_WIDE = 1024
TN_PAIR = 512


def _tile_sizes(m):
    if m >= 1024:
        return dict(tm_wide=1024, tm_ln=512)
    return dict(tm_wide=m, tm_ln=m)


def _trunk(x, mod, mod_kv, s0, attend, weights, *, batch, seq):
    (a_w_in, lb_all, a_norm_g, a_w_out, b_w_q, b_w_o, w_kv, ln_g, ln_b, ffn_w_in, ffn_w_out) = weights
    m = batch * seq
    x = x.reshape(m, D_MODEL)
    ts = _tile_sizes(m)
    tm_wide, tm_ln = ts["tm_wide"], ts["tm_ln"]
    per_row = seq < tm_ln
    rows_per_group = m if per_row else seq

    def vecs(mvec, n):
        out = []
        for part in jnp.split(mvec, n, axis=-1):
            if per_row:
                out.append(jnp.repeat(part, seq, axis=0).reshape(1, m, D_MODEL))
            else:
                out.append(part.reshape(batch, 1, D_MODEL))
        return out

    sh1, sc1, g1, sh2, sc2, g2 = vecs(mod[0], 6)
    sh1b, sc1b, g1b, sh2b, sc2b, g2b = vecs(mod[1], 6)
    shift_kv, scale_kv = vecs(mod_kv, 2)
    common = dict(tm=tm_ln, rows_per_group=rows_per_group)
    narrow = dict(tm_mm=tm_wide, tn=TN_WIDE, **common)
    deep = dict(tm_mm=tm_ln, tn=TN_PAIR, **common)

    h = _modulate(x, sc1, sh1, tm=tm_ln, rows_per_group=rows_per_group)
    proj = _matmul(h, a_w_in, tm=tm_wide, tn=TN_WIDE, name="hgrn_in_proj")
    o, s_fin = _hgrn(proj, lb_all[0], a_norm_g[0], s0[0], batch=batch, seq=seq,
                     out_dtype=BF16 if seq >= HEAD else F32)
    x, (h,) = _matmul_ln(o, a_w_out, x, g1, ln_g[0, 0], ln_b[0, 0], [(sc2, sh2)],
                         name="hgrn_out_proj", **narrow)
    hid = _matmul_pair(h, ffn_w_in, layer=0, swiglu=True, tm=tm_wide, tn=TN_PAIR, name="swiglu_in")
    x, (h, z) = _matmul_ln(hid, ffn_w_out, x, g2, ln_g[0, 1], ln_b[0, 1],
                           [(sc1b, sh1b), (scale_kv, shift_kv)], layer=0,
                           name="ffn0_out_proj", **deep)

    k, v = _matmul_pair(z, w_kv, swiglu=False, tm=tm_wide, tn=TN_PAIR, name="kv_proj")
    q = _matmul(h, b_w_q, scale=HEAD ** -0.5, tm=tm_wide, tn=TN_WIDE, name="q_proj")
    o = attend(q, k, v)
    x, (h,) = _matmul_ln(o, b_w_o, x, g1b, ln_g[1, 0], ln_b[1, 0], [(sc2b, sh2b)],
                         name="moba_out_proj", **narrow)
    hid = _matmul_pair(h, ffn_w_in, layer=1, swiglu=True, tm=tm_wide, tn=TN_PAIR, name="swiglu_in")
    y, _ = _matmul_ln(hid, ffn_w_out, x, g2b, ln_g[1, 1], ln_b[1, 1], [], layer=1,
                      name="ffn1_out_proj", **deep)

    kv_shape = (batch, seq, N_HEADS, HEAD)
    return y.reshape(batch, seq, D_MODEL), s_fin[None], k.reshape(kv_shape), v.reshape(kv_shape)


def kernel(x_prompt, x_sample, state_hgrn, cache_k, cache_v, page_table, c_prompt, c_sample,
           a_w_in, a_lb_logits, a_norm_g, a_w_out, b_w_q, b_w_o, w_ada_kv, b_ada_kv, w_kv,
           w_ada, b_ada, ln_g, ln_b, ffn_w_in, ffn_w_out):
    assert a_w_in.shape[0] == 1 and b_w_q.shape[0] == 1 and w_ada.shape[0] == DEPTH
    bp, tp, _ = x_prompt.shape
    bs, tsamp, _ = x_sample.shape

    lb_all = _lower_bounds(a_lb_logits.astype(F32))

    c_all = jnp.concatenate([c_prompt, c_sample], axis=0)
    n_c = c_all.shape[0]
    c_rows = -(-n_c // BF16_ROWS) * BF16_ROWS
    c_all = jnp.pad(c_all, ((0, c_rows - n_c), (0, 0)))
    ada = functools.partial(_matmul, silu_in=True, tm=c_rows, tn=TN_WIDE)
    mod = [ada(c_all, w_ada, layer=l, bias=b_ada[l], name="ada_layer") for l in range(DEPTH)]
    mod_kv = ada(c_all, w_ada_kv, bias=b_ada_kv, name="ada_kv")

    weights = (a_w_in, lb_all, a_norm_g, a_w_out, b_w_q, b_w_o, w_kv, ln_g, ln_b, ffn_w_in, ffn_w_out)

    s0_prompt = jnp.zeros((a_w_in.shape[0], bp, N_HEADS, HEAD, HEAD), state_hgrn.dtype)
    y_p, s_p, k_p, v_p = _trunk(
        x_prompt, [mm[:bp] for mm in mod], mod_kv[:bp], s0_prompt,
        functools.partial(_moba_seq, batch=bp, seq=tp), weights, batch=bp, seq=tp)
    y_s, s_s, k_s, v_s = _trunk(
        x_sample, [mm[bp:bp + bs] for mm in mod], mod_kv[bp:bp + bs], state_hgrn,
        lambda q, k, v: _moba_paged(q, k, v, cache_k, cache_v, page_table, batch=bs, tq=tsamp),
        weights, batch=bs, seq=tsamp)
    return (y_p, y_s, s_p, s_s, k_p, v_p, k_s, v_s)
```

```python
import functools

import numpy as np
import jax
import jax.numpy as jnp
from jax import lax
from jax.experimental import pallas as pl
from jax.experimental.pallas import tpu as pltpu

F32 = jnp.float32
BF16 = jnp.bfloat16

D_MODEL = 2048
DEPTH = 2
HEAD = 128
N_HEADS = D_MODEL // HEAD
HGRN_PAD_CHUNK = 16
HGRN_HEADS_PER_STEP = 8
MOBA_BLOCK = 256
MOBA_TOPK = 3
MOBA_HEADS_PER_STEP = 4
PAGE_SIZE = 128
PAGES_PER_BLOCK = MOBA_BLOCK // PAGE_SIZE
PAGED_BLOCKS_PER_STEP = 4
BF16_ROWS = 16
LANES = 128
LN_EPS = 1e-5
RMS_EPS = 1e-6
ALPHA = (2.0 * DEPTH) ** 0.25
VMEM_LIMIT_BYTES = 56 * 1024 * 1024

NEG_INF = float("-inf")


def _params():
    return pltpu.CompilerParams(vmem_limit_bytes=VMEM_LIMIT_BYTES)


def _dot(a, b, precision=None):
    return lax.dot_general(a, b, (((1,), (0,)), ((), ())), preferred_element_type=F32,
                           precision=precision)


def _dot_nt(a, b, precision=None):
    return lax.dot_general(a, b, (((1,), (1,)), ((), ())), preferred_element_type=F32,
                           precision=precision)


def _dot_tn(a, b):
    return lax.dot_general(a, b, (((0,), (0,)), ((), ())), preferred_element_type=F32)


def _silu(x):
    return x * jax.nn.sigmoid(x)


def _weight_spec(w, layer, rows, cols, index_map):
    if w.ndim == 2:
        return pl.BlockSpec((rows, cols), index_map)
    return pl.BlockSpec((None, rows, cols), lambda *g: (layer,) + tuple(index_map(*g)))


def _lb_body(logit_ref, out_ref):
    z = logit_ref[...]
    n = z.shape[0]
    m = z[0:1]
    for r in range(1, n):
        m = jnp.maximum(m, z[r:r + 1])
    e = jnp.exp(z - m)
    tot = e[0:1]
    for r in range(1, n):
        tot = tot + e[r:r + 1]
    p = e / tot
    run = p[0:1]
    out_ref[0:1, :] = run
    for r in range(1, n):
        run = run + p[r:r + 1]
        out_ref[r:r + 1, :] = run


def _lower_bounds(logits):
    return pl.pallas_call(
        _lb_body, out_shape=jax.ShapeDtypeStruct(logits.shape, F32), name="lower_bounds")(logits)


def _mm_body(*refs, silu_in, scale, has_bias):
    if has_bias:
        x_ref, w_ref, b_ref, o_ref, wb_ref = refs
    else:
        x_ref, w_ref, o_ref, wb_ref = refs

    @pl.when(pl.program_id(1) == 0)
    def _():
        wb_ref[...] = w_ref[...].astype(BF16)

    x = x_ref[...]
    if silu_in:
        x = _silu(x)
    acc = _dot(x.astype(BF16), wb_ref[...])
    if has_bias:
        acc = acc + b_ref[...]
    if scale is not None:
        acc = acc * scale
    o_ref[...] = acc.astype(o_ref.dtype)


def _matmul(x, w, *, layer=0, bias=None, silu_in=False, scale=None, tm, tn, name):
    m, k = x.shape
    n = w.shape[-1]
    in_specs = [pl.BlockSpec((tm, k), lambda j, i: (i, 0)),
                _weight_spec(w, layer, k, tn, lambda j, i: (0, j))]
    args = [x, w]
    if bias is not None:
        in_specs.append(pl.BlockSpec((1, tn), lambda j, i: (0, j)))
        args.append(bias.reshape(1, n))
    return pl.pallas_call(
        functools.partial(_mm_body, silu_in=silu_in, scale=scale, has_bias=bias is not None),
        grid=(n // tn, m // tm),
        in_specs=in_specs,
        out_specs=pl.BlockSpec((tm, tn), lambda j, i: (i, j)),
        out_shape=jax.ShapeDtypeStruct((m, n), F32),
        scratch_shapes=[pltpu.VMEM((k, tn), BF16)],
        compiler_params=_params(),
        name=name,
    )(*args)


def _pair_body(*refs, swiglu):
    if swiglu:
        x_ref, wa_ref, wu_ref, o_ref, wab_ref, wub_ref = refs
    else:
        x_ref, wa_ref, wu_ref, oa_ref, ou_ref, wab_ref, wub_ref = refs

    @pl.when(pl.program_id(1) == 0)
    def _():
        wab_ref[...] = wa_ref[...].astype(BF16)
        wub_ref[...] = wu_ref[...].astype(BF16)

    x = x_ref[...]
    a = _dot(x, wab_ref[...])
    u = _dot(x, wub_ref[...])
    if swiglu:
        o_ref[...] = (_silu(a) * u).astype(o_ref.dtype)
    else:
        oa_ref[...] = a
        ou_ref[...] = u


def _matmul_pair(x, w, *, layer=0, swiglu, tm, tn, name):
    m, k = x.shape
    f = w.shape[-1] // 2
    nj = f // tn
    out_spec = pl.BlockSpec((tm, tn), lambda j, i: (i, j))
    if swiglu:
        out_specs, out_shape = out_spec, jax.ShapeDtypeStruct((m, f), BF16)
    else:
        out_specs, out_shape = [out_spec] * 2, [jax.ShapeDtypeStruct((m, f), F32)] * 2
    return pl.pallas_call(
        functools.partial(_pair_body, swiglu=swiglu),
        grid=(nj, m // tm),
        in_specs=[pl.BlockSpec((tm, k), lambda j, i: (i, 0)),
                  _weight_spec(w, layer, k, tn, lambda j, i: (0, j)),
                  _weight_spec(w, layer, k, tn, lambda j, i: (0, j + nj))],
        out_specs=out_specs,
        out_shape=out_shape,
        scratch_shapes=[pltpu.VMEM((k, tn), BF16), pltpu.VMEM((k, tn), BF16)],
        compiler_params=_params(),
        name=name,
    )(x, w, w)


def _modulate_body(x_ref, sc_ref, sh_ref, o_ref):
    o_ref[...] = (x_ref[...] * (1.0 + sc_ref[...]) + sh_ref[...]).astype(o_ref.dtype)


def _mod_spec(vec, tm, rows_per_group):
    r = vec.shape[1]
    tiles = rows_per_group // tm
    return pl.BlockSpec((None, r, D_MODEL), lambda i, *_: (i // tiles, 0, 0))


def _modulate(x, sc, sh, *, tm, rows_per_group):
    m = x.shape[0]
    return pl.pallas_call(
        _modulate_body,
        grid=(m // tm,),
        in_specs=[pl.BlockSpec((tm, D_MODEL), lambda i: (i, 0)),
                  _mod_spec(sc, tm, rows_per_group), _mod_spec(sh, tm, rows_per_group)],
        out_specs=pl.BlockSpec((tm, D_MODEL), lambda i: (i, 0)),
        out_shape=jax.ShapeDtypeStruct((m, D_MODEL), BF16),
        compiler_params=_params(),
        name="modulate",
    )(x, sc, sh)


def _ln_body(*refs, n_mod):
    mix_ref, x_ref, gate_ref, lng_ref, lnb_ref = refs[:5]
    mod_refs = refs[5:5 + 2 * n_mod]
    y_ref = refs[5 + 2 * n_mod]
    h_refs = refs[6 + 2 * n_mod:]
    r = ALPHA * x_ref[...] + (1.0 + gate_ref[...]) * mix_ref[...]
    mu = jnp.mean(r, axis=-1, keepdims=True)
    d = r - mu
    var = jnp.mean(d * d, axis=-1, keepdims=True)
    y = d * lax.rsqrt(var + LN_EPS) * lng_ref[...] + lnb_ref[...]
    y_ref[...] = y
    for j in range(n_mod):
        sc_ref, sh_ref = mod_refs[2 * j], mod_refs[2 * j + 1]
        h_refs[j][...] = (y * (1.0 + sc_ref[...]) + sh_ref[...]).astype(BF16)


def _matmul_ln(a, w, x, gate, ln_g, ln_b, mods, *, layer=0, tm_mm, tn, tm, rows_per_group, name):
    mix = _matmul(a, w, layer=layer, tm=tm_mm, tn=tn, name=name)
    m = x.shape[0]
    n_mod = len(mods)
    row_spec = pl.BlockSpec((tm, D_MODEL), lambda i: (i, 0))
    vec_spec = pl.BlockSpec((1, D_MODEL), lambda i: (0, 0))
    in_specs = [row_spec, row_spec, _mod_spec(gate, tm, rows_per_group), vec_spec, vec_spec]
    args = [mix, x, gate, ln_g.reshape(1, D_MODEL), ln_b.reshape(1, D_MODEL)]
    for sc, sh in mods:
        in_specs += [_mod_spec(sc, tm, rows_per_group), _mod_spec(sh, tm, rows_per_group)]
        args += [sc, sh]
    out = pl.pallas_call(
        functools.partial(_ln_body, n_mod=n_mod),
        grid=(m // tm,),
        in_specs=in_specs,
        out_specs=[row_spec] * (1 + n_mod),
        out_shape=[jax.ShapeDtypeStruct((m, D_MODEL), F32)]
        + [jax.ShapeDtypeStruct((m, D_MODEL), BF16)] * n_mod,
        compiler_params=_params(),
        name="residual_ln",
    )(*args)
    return out[0], list(out[1:])


def _hgrn_tables(chunk):
    levels = chunk.bit_length() - 1
    t = np.arange(chunk)[:, None]
    r = np.arange(chunk)[None, :]
    mats = [(r <= t), (r > t)]
    for lv in range(levels):
        size = 1 << lv
        odd = ((t >> lv) & 1) == 1
        start = t & ~(size - 1)
        end = t | (size - 1)
        mats.append(np.where(odd, (r >= start) & (r <= t), (r > t) & (r <= end)))
    tab = np.concatenate(mats, axis=0).astype(np.float32)
    return np.concatenate([tab] * 3, axis=1), levels


def _hgrn_body(q_ref, f_ref, i_ref, g_ref, lb_ref, ng_ref, s0_ref, tab_ref,
               o_ref, sfin_ref, st_ref, *, chunk, valid, levels, n_chunks, heads):
    c = pl.program_id(2)
    hs = range(heads)
    cols = [slice(i * HEAD, (i + 1) * HEAD) for i in hs]

    @pl.when(c == 0)
    def _():
        for i in hs:
            st_ref[i] = s0_ref[i].T

    def pad(a):
        if valid == chunk:
            return a
        return jnp.concatenate([a, jnp.zeros((chunk - valid, HEAD), F32)], axis=0)

    lf, k, q, vb = [], [], [], []
    for i in hs:
        lb = lb_ref[:, cols[i]]
        f = lb + (1.0 - lb) * jax.nn.sigmoid(f_ref[:, cols[i]])
        lf.append(pad(jnp.log(f)))
        k.append(pad(1.0 - f))
        q.append(pad(q_ref[:, cols[i]]))
        vb.append(pad(i_ref[:, cols[i]]).astype(BF16))

    pieces = []
    for i in hs:
        hi = lf[i].astype(BF16)
        r1 = lf[i] - hi.astype(F32)
        mid = r1.astype(BF16)
        lo = (r1 - mid.astype(F32)).astype(BF16)
        pieces.append(jnp.concatenate([hi, mid, lo], axis=0))
    e_all = _dot(tab_ref[...], jnp.concatenate(pieces, axis=1))

    def table(t, i):
        return e_all[t * chunk:(t + 1) * chunk, cols[i]]

    row = lax.broadcasted_iota(jnp.int32, (chunk, chunk), 0)
    col = lax.broadcasted_iota(jnp.int32, (chunk, chunk), 1)
    trow = lax.broadcasted_iota(jnp.int32, (chunk, 1), 0)

    attn = [jnp.where(row == col, _dot_nt(q[i].astype(BF16), k[i].astype(BF16)), 0.0) for i in hs]
    for lv in range(levels):
        odd = ((trow >> lv) & 1) == 1
        pair = ((row >> (lv + 1)) == (col >> (lv + 1))) & (((row >> lv) & 1) == 1) \
            & (((col >> lv) & 1) == 0)
        for i in hs:
            ex = jnp.exp(table(2 + lv, i))
            qs = jnp.where(odd, q[i] * ex, 0.0).astype(BF16)
            ks = jnp.where(odd, 0.0, k[i] * ex).astype(BF16)
            attn[i] = attn[i] + jnp.where(pair, _dot_nt(qs, ks), 0.0)

    for i in hs:
        b = table(0, i)
        st = st_ref[i]
        o = _dot_nt((q[i] * jnp.exp(b)).astype(BF16), st.astype(BF16)) \
            + _dot(attn[i].astype(BF16), vb[i])
        b_last = b[chunk - 1:chunk]
        st_new = jnp.exp(b_last) * st + _dot_tn(vb[i], (k[i] * jnp.exp(table(1, i))).astype(BF16))
        st_ref[i] = st_new

        o = o[0:valid]
        o = o * lax.rsqrt(jnp.mean(o * o, axis=-1, keepdims=True) + RMS_EPS) * ng_ref[:, cols[i]]
        o_ref[:, cols[i]] = (o * _silu(g_ref[:, cols[i]])).astype(o_ref.dtype)

    @pl.when(c == n_chunks - 1)
    def _():
        for i in hs:
            sfin_ref[i] = st_ref[i].T


def _hgrn(proj, lb, norm_g, s0, *, batch, seq, out_dtype):
    if seq >= HEAD:
        chunk = valid = HEAD
        heads = HGRN_HEADS_PER_STEP
    else:
        chunk, valid = HGRN_PAD_CHUNK, seq
        heads = N_HEADS
    n_chunks = seq // valid
    groups = N_HEADS // heads
    tables, levels = _hgrn_tables(chunk)
    tables = jnp.asarray(tables, BF16)
    width = heads * HEAD

    def col_spec(part):
        return pl.BlockSpec((valid, width), lambda b, h, c: (b * n_chunks + c, part * groups + h))

    head_vec = pl.BlockSpec((1, width), lambda b, h, c: (0, h))
    state_spec = pl.BlockSpec((None, heads, HEAD, HEAD), lambda b, h, c: (b, h, 0, 0))
    o, s_fin = pl.pallas_call(
        functools.partial(_hgrn_body, chunk=chunk, valid=valid, levels=levels,
                          n_chunks=n_chunks, heads=heads),
        grid=(batch, groups, n_chunks),
        in_specs=[col_spec(0), col_spec(1), col_spec(2), col_spec(3), head_vec, head_vec,
                  state_spec, pl.BlockSpec(tables.shape, lambda b, h, c: (0, 0))],
        out_specs=[pl.BlockSpec((valid, width), lambda b, h, c: (b * n_chunks + c, h)), state_spec],
        out_shape=[jax.ShapeDtypeStruct((batch * seq, D_MODEL), out_dtype),
                   jax.ShapeDtypeStruct((batch, N_HEADS, HEAD, HEAD), F32)],
        scratch_shapes=[pltpu.VMEM((heads, HEAD, HEAD), F32)],
        compiler_params=_params(),
        name="hgrn",
    )(proj, proj, proj, proj, lb.reshape(1, D_MODEL), norm_g.reshape(1, D_MODEL), s0, tables)
    return o, s_fin


def _select_bias(gate, n_valid_blocks, axis):
    nb = gate.shape[axis]
    blk = lax.broadcasted_iota(jnp.int32, gate.shape, axis)
    valid = blk < n_valid_blocks
    beaten = jnp.zeros(gate.shape, jnp.int32)
    for m in range(min(nb, n_valid_blocks)):
        gm = gate[m:m + 1, :] if axis == 0 else gate[:, m:m + 1]
        wins = (gm > gate) | ((gm == gate) & (m < blk))
        beaten = beaten + jnp.where(wins, 1, 0)
    return jnp.where(valid & (beaten < MOBA_TOPK), 0.0, NEG_INF)


def _moba_seq_body(q_ref, k_ref, v_ref, o_ref, kb_ref, vt_ref, kmean_ref, *, nb, heads):
    i = pl.program_id(2)
    hs = range(heads)
    cols = [slice(h * HEAD, (h + 1) * HEAD) for h in hs]

    @pl.when(i == 0)
    def _():
        for h in hs:
            kf = k_ref[:, cols[h]]
            kb_ref[h] = kf.astype(BF16)
            vt_ref[h] = v_ref[:, cols[h]].T.astype(BF16)
            means = [jnp.mean(kf[n * MOBA_BLOCK:(n + 1) * MOBA_BLOCK], axis=0, keepdims=True)
                     for n in range(nb)]
            kmean_ref[h] = jnp.concatenate(means, axis=0)

    qt = [q_ref[:, cols[h]].T for h in hs]
    qtb = [a.astype(BF16) for a in qt]
    gate = [_dot(kmean_ref[h], qt[h], precision=lax.Precision.HIGHEST) for h in hs]
    key = lax.broadcasted_iota(jnp.int32, (MOBA_BLOCK, MOBA_BLOCK), 0)
    qry = lax.broadcasted_iota(jnp.int32, (MOBA_BLOCK, MOBA_BLOCK), 1)

    def attend(own):
        n_keys = (own + 1) * MOBA_BLOCK
        bias = [_select_bias(gate[h], own, axis=0) for h in hs]
        s_all = [_dot(kb_ref[h, 0:n_keys, :], qtb[h]) for h in hs]
        s, m = [], []
        for h in hs:
            sh = [s_all[h][n * MOBA_BLOCK:(n + 1) * MOBA_BLOCK] + bias[h][n:n + 1]
                  for n in range(own)]
            sh.append(jnp.where(key <= qry, s_all[h][own * MOBA_BLOCK:n_keys], NEG_INF))
            mh = jnp.max(sh[own], axis=0, keepdims=True)
            for n in range(own):
                mh = jnp.maximum(mh, jnp.max(sh[n], axis=0, keepdims=True))
            s.append(sh)
            m.append(mh)
        for h in hs:
            l = jnp.zeros_like(m[h])
            ps = []
            for n in range(own + 1):
                p = jnp.exp(s[h][n] - m[h])
                l = l + jnp.sum(p, axis=0, keepdims=True)
                ps.append(p.astype(BF16))
            acc = _dot(vt_ref[h, :, 0:n_keys], jnp.concatenate(ps, axis=0))
            o_ref[:, cols[h]] = (acc / l).T.astype(o_ref.dtype)

    for own in range(nb):
        pl.when(i == own)(functools.partial(attend, own))


def _moba_seq(q, k, v, *, batch, seq):
    nb = seq // MOBA_BLOCK
    heads = MOBA_HEADS_PER_STEP
    width = heads * HEAD
    return pl.pallas_call(
        functools.partial(_moba_seq_body, nb=nb, heads=heads),
        grid=(batch, N_HEADS // heads, nb),
        in_specs=[pl.BlockSpec((MOBA_BLOCK, width), lambda b, h, i: (b * nb + i, h)),
                  pl.BlockSpec((seq, width), lambda b, h, i: (b, h)),
                  pl.BlockSpec((seq, width), lambda b, h, i: (b, h))],
        out_specs=pl.BlockSpec((MOBA_BLOCK, width), lambda b, h, i: (b * nb + i, h)),
        out_shape=jax.ShapeDtypeStruct((batch * seq, D_MODEL), BF16),
        scratch_shapes=[pltpu.VMEM((heads, seq, HEAD), BF16), pltpu.VMEM((heads, HEAD, seq), BF16),
                        pltpu.VMEM((heads, nb, HEAD), F32)],
        compiler_params=_params(),
        name="moba_prompt",
    )(q, k, v)


def _moba_paged_body(pt_ref, q_ref, kn_ref, vn_ref, *refs, nb, tq):
    del pt_ref
    n_page_refs = PAGED_BLOCKS_PER_STEP * PAGES_PER_BLOCK
    k_refs, v_refs = refs[:n_page_refs], refs[n_page_refs:2 * n_page_refs]
    o_ref, qs_ref, hmask_ref, kmean_ref, m_ref, l_ref, part_ref = refs[2 * n_page_refs:]
    step = pl.program_id(1)
    n_steps = nb // PAGED_BLOCKS_PER_STEP
    rows = N_HEADS * tq
    page_rows = PAGE_SIZE * N_HEADS
    head_shift = N_HEADS.bit_length() - 1
    tq_shift = tq.bit_length() - 1

    @pl.when(step == 0)
    def _():
        q = q_ref[...]
        qs_ref[...] = jnp.concatenate([q[:, h * HEAD:(h + 1) * HEAD] for h in range(N_HEADS)],
                                      axis=0)
        r = lax.broadcasted_iota(jnp.int32, (rows, page_rows), 0)
        cc = lax.broadcasted_iota(jnp.int32, (rows, page_rows), 1)
        hmask_ref[...] = jnp.where((cc & (N_HEADS - 1)) == (r >> tq_shift), 0.0, NEG_INF)
        m_ref[...] = jnp.zeros_like(m_ref)
        l_ref[...] = jnp.zeros_like(l_ref)

    qsb = qs_ref[...].astype(BF16)
    hmask = hmask_ref[...]
    lane = lax.broadcasted_iota(jnp.int32, (rows, LANES), 1)
    ones = jnp.ones((page_rows, HEAD), BF16)

    for bi in range(PAGED_BLOCKS_PER_STEP):
        n = step * PAGED_BLOCKS_PER_STEP + bi
        k_pages = k_refs[bi * PAGES_PER_BLOCK:(bi + 1) * PAGES_PER_BLOCK]
        v_pages = v_refs[bi * PAGES_PER_BLOCK:(bi + 1) * PAGES_PER_BLOCK]

        ksum = jnp.sum(k_pages[0][...], axis=0)
        for kp in k_pages[1:]:
            ksum = ksum + jnp.sum(kp[...], axis=0)
        kmean_ref[n] = ksum * (1.0 / MOBA_BLOCK)

        for pg, (kp, vp) in enumerate(zip(k_pages, v_pages)):
            s = _dot_nt(qsb, kp[...].reshape(page_rows, HEAD).astype(BF16)) + hmask
            tiles = [s[:, c * LANES:(c + 1) * LANES] for c in range(page_rows // LANES)]
            m_n = jnp.max(functools.reduce(jnp.maximum, tiles), axis=-1, keepdims=True)
            p = jnp.exp(s - m_n).astype(BF16)
            v_aug = jnp.concatenate([vp[...].reshape(page_rows, HEAD).astype(BF16), ones], axis=1)
            o_aug = _dot(p, v_aug)
            part_ref[pg, n] = o_aug[:, 0:HEAD]
            m_ref[pg] = jnp.where(lane == n, m_n, m_ref[pg])
            l_ref[pg] = jnp.where(lane == n, o_aug[:, HEAD:HEAD + 1], l_ref[pg])

    @pl.when(step == n_steps - 1)
    def _():
        qs = qs_ref[...]
        gates = [_dot_nt(qs[h * tq:(h + 1) * tq], kmean_ref[:, h, :], precision=lax.Precision.HIGHEST)
                 for h in range(N_HEADS)]
        bias = _select_bias(jnp.concatenate(gates, axis=0), nb, axis=1)

        s_own = _dot_nt(qsb, kn_ref[...].astype(BF16))
        r = lax.broadcasted_iota(jnp.int32, s_own.shape, 0)
        cc = lax.broadcasted_iota(jnp.int32, s_own.shape, 1)
        keep = ((cc & (N_HEADS - 1)) == (r >> tq_shift)) & ((cc >> head_shift) <= (r & (tq - 1)))
        s_own = jnp.where(keep, s_own, NEG_INF)
        m_own = jnp.max(s_own, axis=-1, keepdims=True)
        p_own = jnp.exp(s_own - m_own)
        l_own = jnp.sum(p_own, axis=-1, keepdims=True)
        o_own = _dot(p_own.astype(BF16), vn_ref[...].astype(BF16))

        mb = [m_ref[pg][:, 0:nb] + bias for pg in range(PAGES_PER_BLOCK)]
        m = m_own
        for mbp in mb:
            m = jnp.maximum(m, jnp.max(mbp, axis=-1, keepdims=True))
        w_own = jnp.exp(m_own - m)
        l = w_own * l_own
        o = w_own * o_own
        for pg in range(PAGES_PER_BLOCK):
            w = jnp.exp(mb[pg] - m)
            l = l + jnp.sum(w * l_ref[pg][:, 0:nb], axis=-1, keepdims=True)
            for j in range(nb):
                o = o + w[:, j:j + 1] * part_ref[pg, j]
        o = o / l
        for h in range(N_HEADS):
            o_ref[:, h * HEAD:(h + 1) * HEAD] = o[h * tq:(h + 1) * tq]


def _moba_paged(q, k_new, v_new, cache_k, cache_v, page_table, *, batch, tq):
    n_pages = page_table.shape[1]
    nb = n_pages // PAGES_PER_BLOCK
    rows = N_HEADS * tq
    assert nb <= LANES and tq & (tq - 1) == 0 and rows % 8 == 0
    assert nb % PAGED_BLOCKS_PER_STEP == 0
    page_block = (None, PAGE_SIZE, N_HEADS, HEAD)
    pages_per_step = PAGED_BLOCKS_PER_STEP * PAGES_PER_BLOCK

    def page(p):
        return pl.BlockSpec(page_block, lambda b, n, pt: (pt[b, pages_per_step * n + p], 0, 0, 0))

    pages = [page(p) for p in range(pages_per_step)]
    q_spec = pl.BlockSpec((tq, D_MODEL), lambda b, n, pt: (b, 0))
    new_spec = pl.BlockSpec((rows, HEAD), lambda b, n, pt: (b, 0))
    grid_spec = pltpu.PrefetchScalarGridSpec(
        num_scalar_prefetch=1,
        grid=(batch, nb // PAGED_BLOCKS_PER_STEP),
        in_specs=[q_spec, new_spec, new_spec] + pages + pages,
        out_specs=q_spec,
        scratch_shapes=[pltpu.VMEM((rows, HEAD), F32),
                        pltpu.VMEM((rows, PAGE_SIZE * N_HEADS), F32),
                        pltpu.VMEM((nb, N_HEADS, HEAD), F32),
                        pltpu.VMEM((PAGES_PER_BLOCK, rows, LANES), F32),
                        pltpu.VMEM((PAGES_PER_BLOCK, rows, LANES), F32),
                        pltpu.VMEM((PAGES_PER_BLOCK, nb, rows, HEAD), F32)],
    )
    k_rows = k_new.reshape(batch * rows, HEAD)
    v_rows = v_new.reshape(batch * rows, HEAD)
    return pl.pallas_call(
        functools.partial(_moba_paged_body, nb=nb, tq=tq),
        grid_spec=grid_spec,
        out_shape=jax.ShapeDtypeStruct((batch * tq, D_MODEL), F32),
        compiler_params=_params(),
        name="moba_paged",
    )(page_table, q, k_rows, v_rows, *([cache_k] * pages_per_step), *([cache_v] * pages_per_step))


TN_WIDE = 1024
TN_PAIR = 512


def _tile_sizes(m):
    if m >= 1024:
        return dict(tm_wide=1024, tm_ln=512)
    return dict(tm_wide=m, tm_ln=m)


def _trunk(x, mod, mod_kv, s0, attend, weights, *, batch, seq):
    (a_w_in, lb_all, a_norm_g, a_w_out, b_w_q, b_w_o, w_kv, ln_g, ln_b, ffn_w_in, ffn_w_out) = weights
    m = batch * seq
    x = x.reshape(m, D_MODEL)
    ts = _tile_sizes(m)
    tm_wide, tm_ln = ts["tm_wide"], ts["tm_ln"]
    per_row = seq < tm_ln
    rows_per_group = m if per_row else seq

    def vecs(mvec, n):
        out = []
        for part in jnp.split(mvec, n, axis=-1):
            if per_row:
                out.append(jnp.repeat(part, seq, axis=0).reshape(1, m, D_MODEL))
            else:
                out.append(part.reshape(batch, 1, D_MODEL))
        return out

    sh1, sc1, g1, sh2, sc2, g2 = vecs(mod[0], 6)
    sh1b, sc1b, g1b, sh2b, sc2b, g2b = vecs(mod[1], 6)
    shift_kv, scale_kv = vecs(mod_kv, 2)
    common = dict(tm=tm_ln, rows_per_group=rows_per_group)
    narrow = dict(tm_mm=tm_wide, tn=TN_WIDE, **common)
    deep = dict(tm_mm=tm_ln, tn=TN_PAIR, **common)

    h = _modulate(x, sc1, sh1, tm=tm_ln, rows_per_group=rows_per_group)
    proj = _matmul(h, a_w_in, tm=tm_wide, tn=TN_WIDE, name="hgrn_in_proj")
    o, s_fin = _hgrn(proj, lb_all[0], a_norm_g[0], s0[0], batch=batch, seq=seq,
                     out_dtype=BF16 if seq >= HEAD else F32)
    x, (h,) = _matmul_ln(o, a_w_out, x, g1, ln_g[0, 0], ln_b[0, 0], [(sc2, sh2)],
                         name="hgrn_out_proj", **narrow)
    hid = _matmul_pair(h, ffn_w_in, layer=0, swiglu=True, tm=tm_wide, tn=TN_PAIR, name="swiglu_in")
    x, (h, z) = _matmul_ln(hid, ffn_w_out, x, g2, ln_g[0, 1], ln_b[0, 1],
                           [(sc1b, sh1b), (scale_kv, shift_kv)], layer=0,
                           name="ffn0_out_proj", **deep)

    k, v = _matmul_pair(z, w_kv, swiglu=False, tm=tm_wide, tn=TN_PAIR, name="kv_proj")
    q = _matmul(h, b_w_q, scale=HEAD ** -0.5, tm=tm_wide, tn=TN_WIDE, name="q_proj")
    o = attend(q, k, v)
    x, (h,) = _matmul_ln(o, b_w_o, x, g1b, ln_g[1, 0], ln_b[1, 0], [(sc2b, sh2b)],
                         name="moba_out_proj", **narrow)
    hid = _matmul_pair(h, ffn_w_in, layer=1, swiglu=True, tm=tm_wide, tn=TN_PAIR, name="swiglu_in")
    y, _ = _matmul_ln(hid, ffn_w_out, x, g2b, ln_g[1, 1], ln_b[1, 1], [], layer=1,
                      name="ffn1_out_proj", **deep)

    kv_shape = (batch, seq, N_HEADS, HEAD)
    return y.reshape(batch, seq, D_MODEL), s_fin[None], k.reshape(kv_shape), v.reshape(kv_shape)


def kernel(x_prompt, x_sample, state_hgrn, cache_k, cache_v, page_table, c_prompt, c_sample,
           a_w_in, a_lb_logits, a_norm_g, a_w_out, b_w_q, b_w_o, w_ada_kv, b_ada_kv, w_kv,
           w_ada, b_ada, ln_g, ln_b, ffn_w_in, ffn_w_out):
    assert a_w_in.shape[0] == 1 and b_w_q.shape[0] == 1 and w_ada.shape[0] == DEPTH
    bp, tp, _ = x_prompt.shape
    bs, tsamp, _ = x_sample.shape

    lb_all = _lower_bounds(a_lb_logits.astype(F32))

    c_all = jnp.concatenate([c_prompt, c_sample], axis=0)
    n_c = c_all.shape[0]
    c_rows = -(-n_c // BF16_ROWS) * BF16_ROWS
    c_all = jnp.pad(c_all, ((0, c_rows - n_c), (0, 0)))
    ada = functools.partial(_matmul, silu_in=True, tm=c_rows, tn=TN_WIDE)
    mod = [ada(c_all, w_ada, layer=l, bias=b_ada[l], name="ada_layer") for l in range(DEPTH)]
    mod_kv = ada(c_all, w_ada_kv, bias=b_ada_kv, name="ada_kv")

    weights = (a_w_in, lb_all, a_norm_g, a_w_out, b_w_q, b_w_o, w_kv, ln_g, ln_b, ffn_w_in, ffn_w_out)

    s0_prompt = jnp.zeros((a_w_in.shape[0], bp, N_HEADS, HEAD, HEAD), state_hgrn.dtype)
    y_p, s_p, k_p, v_p = _trunk(
        x_prompt, [mm[:bp] for mm in mod], mod_kv[:bp], s0_prompt,
        functools.partial(_moba_seq, batch=bp, seq=tp), weights, batch=bp, seq=tp)
    y_s, s_s, k_s, v_s = _trunk(
        x_sample, [mm[bp:bp + bs] for mm in mod], mod_kv[bp:bp + bs], state_hgrn,
        lambda q, k, v: _moba_paged(q, k, v, cache_k, cache_v, page_table, batch=bs, tq=tsamp),
        weights, batch=bs, seq=tsamp)
    return (y_p, y_s, s_p, s_s, k_p, v_p, k_s, v_s)
```

```python
import functools

import numpy as np
import jax
import jax.numpy as jnp
from jax import lax
from jax.experimental import pallas as pl
from jax.experimental.pallas import tpu as pltpu

F32 = jnp.float32
BF16 = jnp.bfloat16

D_MODEL = 2048
DEPTH = 2
HEAD = 128
N_HEADS = D_MODEL // HEAD
HGRN_PAD_CHUNK = 16
HGRN_HEADS_PER_STEP = 16
MOBA_BLOCK = 256
MOBA_TOPK = 3
MOBA_HEADS_PER_STEP = 4
PAGE_SIZE = 128
PAGES_PER_BLOCK = MOBA_BLOCK // PAGE_SIZE
PAGED_BLOCKS_PER_STEP = 4
BF16_ROWS = 16
LANES = 128
LN_EPS = 1e-5
RMS_EPS = 1e-6
ALPHA = (2.0 * DEPTH) ** 0.25
VMEM_LIMIT_BYTES = 56 * 1024 * 1024

NEG_INF = float("-inf")


def _params():
    return pltpu.CompilerParams(vmem_limit_bytes=VMEM_LIMIT_BYTES)


def _dot(a, b, precision=None):
    return lax.dot_general(a, b, (((1,), (0,)), ((), ())), preferred_element_type=F32,
                           precision=precision)


def _dot_nt(a, b, precision=None):
    return lax.dot_general(a, b, (((1,), (1,)), ((), ())), preferred_element_type=F32,
                           precision=precision)


def _dot_tn(a, b):
    return lax.dot_general(a, b, (((0,), (0,)), ((), ())), preferred_element_type=F32)


def _silu(x):
    return x * jax.nn.sigmoid(x)


def _weight_spec(w, layer, rows, cols, index_map):
    if w.ndim == 2:
        return pl.BlockSpec((rows, cols), index_map)
    return pl.BlockSpec((None, rows, cols), lambda *g: (layer,) + tuple(index_map(*g)))


def _lb_body(logit_ref, out_ref):
    z = logit_ref[...]
    n = z.shape[0]
    m = z[0:1]
    for r in range(1, n):
        m = jnp.maximum(m, z[r:r + 1])
    e = jnp.exp(z - m)
    tot = e[0:1]
    for r in range(1, n):
        tot = tot + e[r:r + 1]
    p = e / tot
    run = p[0:1]
    out_ref[0:1, :] = run
    for r in range(1, n):
        run = run + p[r:r + 1]
        out_ref[r:r + 1, :] = run


def _lower_bounds(logits):
    return pl.pallas_call(
        _lb_body, out_shape=jax.ShapeDtypeStruct(logits.shape, F32), name="lower_bounds")(logits)


def _mm_body(*refs, silu_in, scale, has_bias):
    if has_bias:
        x_ref, w_ref, b_ref, o_ref, wb_ref = refs
    else:
        x_ref, w_ref, o_ref, wb_ref = refs

    @pl.when(pl.program_id(1) == 0)
    def _():
        wb_ref[...] = w_ref[...].astype(BF16)

    x = x_ref[...]
    if silu_in:
        x = _silu(x)
    acc = _dot(x.astype(BF16), wb_ref[...])
    if has_bias:
        acc = acc + b_ref[...]
    if scale is not None:
        acc = acc * scale
    o_ref[...] = acc.astype(o_ref.dtype)


def _matmul(x, w, *, layer=0, bias=None, silu_in=False, scale=None, tm, tn, name):
    m, k = x.shape
    n = w.shape[-1]
    in_specs = [pl.BlockSpec((tm, k), lambda j, i: (i, 0)),
                _weight_spec(w, layer, k, tn, lambda j, i: (0, j))]
    args = [x, w]
    if bias is not None:
        in_specs.append(pl.BlockSpec((1, tn), lambda j, i: (0, j)))
        args.append(bias.reshape(1, n))
    return pl.pallas_call(
        functools.partial(_mm_body, silu_in=silu_in, scale=scale, has_bias=bias is not None),
        grid=(n // tn, m // tm),
        in_specs=in_specs,
        out_specs=pl.BlockSpec((tm, tn), lambda j, i: (i, j)),
        out_shape=jax.ShapeDtypeStruct((m, n), F32),
        scratch_shapes=[pltpu.VMEM((k, tn), BF16)],
        compiler_params=_params(),
        name=name,
    )(*args)


def _pair_body(*refs, swiglu):
    if swiglu:
        x_ref, wa_ref, wu_ref, o_ref, wab_ref, wub_ref = refs
    else:
        x_ref, wa_ref, wu_ref, oa_ref, ou_ref, wab_ref, wub_ref = refs

    @pl.when(pl.program_id(1) == 0)
    def _():
        wab_ref[...] = wa_ref[...].astype(BF16)
        wub_ref[...] = wu_ref[...].astype(BF16)

    x = x_ref[...]
    a = _dot(x, wab_ref[...])
    u = _dot(x, wub_ref[...])
    if swiglu:
        o_ref[...] = (_silu(a) * u).astype(o_ref.dtype)
    else:
        oa_ref[...] = a
        ou_ref[...] = u


def _matmul_pair(x, w, *, layer=0, swiglu, tm, tn, name):
    m, k = x.shape
    f = w.shape[-1] // 2
    nj = f // tn
    out_spec = pl.BlockSpec((tm, tn), lambda j, i: (i, j))
    if swiglu:
        out_specs, out_shape = out_spec, jax.ShapeDtypeStruct((m, f), BF16)
    else:
        out_specs, out_shape = [out_spec] * 2, [jax.ShapeDtypeStruct((m, f), F32)] * 2
    return pl.pallas_call(
        functools.partial(_pair_body, swiglu=swiglu),
        grid=(nj, m // tm),
        in_specs=[pl.BlockSpec((tm, k), lambda j, i: (i, 0)),
                  _weight_spec(w, layer, k, tn, lambda j, i: (0, j)),
                  _weight_spec(w, layer, k, tn, lambda j, i: (0, j + nj))],
        out_specs=out_specs,
        out_shape=out_shape,
        scratch_shapes=[pltpu.VMEM((k, tn), BF16), pltpu.VMEM((k, tn), BF16)],
        compiler_params=_params(),
        name=name,
    )(x, w, w)


def _modulate_body(x_ref, sc_ref, sh_ref, o_ref):
    o_ref[...] = (x_ref[...] * (1.0 + sc_ref[...]) + sh_ref[...]).astype(o_ref.dtype)


def _mod_spec(vec, tm, rows_per_group):
    r = vec.shape[1]
    tiles = rows_per_group // tm
    return pl.BlockSpec((None, r, D_MODEL), lambda i, *_: (i // tiles, 0, 0))


def _modulate(x, sc, sh, *, tm, rows_per_group):
    m = x.shape[0]
    return pl.pallas_call(
        _modulate_body,
        grid=(m // tm,),
        in_specs=[pl.BlockSpec((tm, D_MODEL), lambda i: (i, 0)),
                  _mod_spec(sc, tm, rows_per_group), _mod_spec(sh, tm, rows_per_group)],
        out_specs=pl.BlockSpec((tm, D_MODEL), lambda i: (i, 0)),
        out_shape=jax.ShapeDtypeStruct((m, D_MODEL), BF16),
        compiler_params=_params(),
        name="modulate",
    )(x, sc, sh)


def _ln_body(*refs, n_mod):
    mix_ref, x_ref, gate_ref, lng_ref, lnb_ref = refs[:5]
    mod_refs = refs[5:5 + 2 * n_mod]
    y_ref = refs[5 + 2 * n_mod]
    h_refs = refs[6 + 2 * n_mod:]
    r = ALPHA * x_ref[...] + (1.0 + gate_ref[...]) * mix_ref[...]
    mu = jnp.mean(r, axis=-1, keepdims=True)
    d = r - mu
    var = jnp.mean(d * d, axis=-1, keepdims=True)
    y = d * lax.rsqrt(var + LN_EPS) * lng_ref[...] + lnb_ref[...]
    y_ref[...] = y
    for j in range(n_mod):
        sc_ref, sh_ref = mod_refs[2 * j], mod_refs[2 * j + 1]
        h_refs[j][...] = (y * (1.0 + sc_ref[...]) + sh_ref[...]).astype(BF16)


def _matmul_ln(a, w, x, gate, ln_g, ln_b, mods, *, layer=0, tm_mm, tn, tm, rows_per_group, name):
    mix = _matmul(a, w, layer=layer, tm=tm_mm, tn=tn, name=name)
    m = x.shape[0]
    n_mod = len(mods)
    row_spec = pl.BlockSpec((tm, D_MODEL), lambda i: (i, 0))
    vec_spec = pl.BlockSpec((1, D_MODEL), lambda i: (0, 0))
    in_specs = [row_spec, row_spec, _mod_spec(gate, tm, rows_per_group), vec_spec, vec_spec]
    args = [mix, x, gate, ln_g.reshape(1, D_MODEL), ln_b.reshape(1, D_MODEL)]
    for sc, sh in mods:
        in_specs += [_mod_spec(sc, tm, rows_per_group), _mod_spec(sh, tm, rows_per_group)]
        args += [sc, sh]
    out = pl.pallas_call(
        functools.partial(_ln_body, n_mod=n_mod),
        grid=(m // tm,),
        in_specs=in_specs,
        out_specs=[row_spec] * (1 + n_mod),
        out_shape=[jax.ShapeDtypeStruct((m, D_MODEL), F32)]
        + [jax.ShapeDtypeStruct((m, D_MODEL), BF16)] * n_mod,
        compiler_params=_params(),
        name="residual_ln",
    )(*args)
    return out[0], list(out[1:])


def _hgrn_tables(chunk):
    levels = chunk.bit_length() - 1
    t = np.arange(chunk)[:, None]
    r = np.arange(chunk)[None, :]
    mats = [(r <= t), (r > t)]
    for lv in range(levels):
        size = 1 << lv
        odd = ((t >> lv) & 1) == 1
        start = t & ~(size - 1)
        end = t | (size - 1)
        mats.append(np.where(odd, (r >= start) & (r <= t), (r > t) & (r <= end)))
    tab = np.concatenate(mats, axis=0).astype(np.float32)
    return np.concatenate([tab] * 3, axis=1), levels


def _hgrn_body(q_ref, f_ref, i_ref, g_ref, lb_ref, ng_ref, s0_ref, tab_ref,
               o_ref, sfin_ref, st_ref, *, chunk, valid, levels, n_chunks, heads):
    c = pl.program_id(2)
    hs = range(heads)
    cols = [slice(i * HEAD, (i + 1) * HEAD) for i in hs]

    @pl.when(c == 0)
    def _():
        for i in hs:
            st_ref[i] = s0_ref[i].T

    def pad(a):
        if valid == chunk:
            return a
        return jnp.concatenate([a, jnp.zeros((chunk - valid, HEAD), F32)], axis=0)

    lf, k, q, vb = [], [], [], []
    for i in hs:
        lb = lb_ref[:, cols[i]]
        f = lb + (1.0 - lb) * jax.nn.sigmoid(f_ref[:, cols[i]])
        lf.append(pad(jnp.log(f)))
        k.append(pad(1.0 - f))
        q.append(pad(q_ref[:, cols[i]]))
        vb.append(pad(i_ref[:, cols[i]]).astype(BF16))

    pieces = []
    for i in hs:
        hi = lf[i].astype(BF16)
        r1 = lf[i] - hi.astype(F32)
        mid = r1.astype(BF16)
        lo = (r1 - mid.astype(F32)).astype(BF16)
        pieces.append(jnp.concatenate([hi, mid, lo], axis=0))
    e_all = _dot(tab_ref[...], jnp.concatenate(pieces, axis=1))

    def table(t, i):
        return e_all[t * chunk:(t + 1) * chunk, cols[i]]

    row = lax.broadcasted_iota(jnp.int32, (chunk, chunk), 0)
    col = lax.broadcasted_iota(jnp.int32, (chunk, chunk), 1)
    trow = lax.broadcasted_iota(jnp.int32, (chunk, 1), 0)

    attn = [jnp.where(row == col, _dot_nt(q[i].astype(BF16), k[i].astype(BF16)), 0.0) for i in hs]
    for lv in range(levels):
        odd = ((trow >> lv) & 1) == 1
        pair = ((row >> (lv + 1)) == (col >> (lv + 1))) & (((row >> lv) & 1) == 1) \
            & (((col >> lv) & 1) == 0)
        for i in hs:
            ex = jnp.exp(table(2 + lv, i))
            qs = jnp.where(odd, q[i] * ex, 0.0).astype(BF16)
            ks = jnp.where(odd, 0.0, k[i] * ex).astype(BF16)
            attn[i] = attn[i] + jnp.where(pair, _dot_nt(qs, ks), 0.0)

    for i in hs:
        b = table(0, i)
        st = st_ref[i]
        o = _dot_nt((q[i] * jnp.exp(b)).astype(BF16), st.astype(BF16)) \
            + _dot(attn[i].astype(BF16), vb[i])
        b_last = b[chunk - 1:chunk]
        st_new = jnp.exp(b_last) * st + _dot_tn(vb[i], (k[i] * jnp.exp(table(1, i))).astype(BF16))
        st_ref[i] = st_new

        o = o[0:valid]
        o = o * lax.rsqrt(jnp.mean(o * o, axis=-1, keepdims=True) + RMS_EPS) * ng_ref[:, cols[i]]
        o_ref[:, cols[i]] = (o * _silu(g_ref[:, cols[i]])).astype(o_ref.dtype)

    @pl.when(c == n_chunks - 1)
    def _():
        for i in hs:
            sfin_ref[i] = st_ref[i].T


def _hgrn(proj, lb, norm_g, s0, *, batch, seq, out_dtype):
    if seq >= HEAD:
        chunk = valid = HEAD
        heads = HGRN_HEADS_PER_STEP
    else:
        chunk, valid = HGRN_PAD_CHUNK, seq
        heads = N_HEADS
    n_chunks = seq // valid
    groups = N_HEADS // heads
    tables, levels = _hgrn_tables(chunk)
    tables = jnp.asarray(tables, BF16)
    width = heads * HEAD

    def col_spec(part):
        return pl.BlockSpec((valid, width), lambda b, h, c: (b * n_chunks + c, part * groups + h))

    head_vec = pl.BlockSpec((1, width), lambda b, h, c: (0, h))
    state_spec = pl.BlockSpec((None, heads, HEAD, HEAD), lambda b, h, c: (b, h, 0, 0))
    o, s_fin = pl.pallas_call(
        functools.partial(_hgrn_body, chunk=chunk, valid=valid, levels=levels,
                          n_chunks=n_chunks, heads=heads),
        grid=(batch, groups, n_chunks),
        in_specs=[col_spec(0), col_spec(1), col_spec(2), col_spec(3), head_vec, head_vec,
                  state_spec, pl.BlockSpec(tables.shape, lambda b, h, c: (0, 0))],
        out_specs=[pl.BlockSpec((valid, width), lambda b, h, c: (b * n_chunks + c, h)), state_spec],
        out_shape=[jax.ShapeDtypeStruct((batch * seq, D_MODEL), out_dtype),
                   jax.ShapeDtypeStruct((batch, N_HEADS, HEAD, HEAD), F32)],
        scratch_shapes=[pltpu.VMEM((heads, HEAD, HEAD), F32)],
        compiler_params=_params(),
        name="hgrn",
    )(proj, proj, proj, proj, lb.reshape(1, D_MODEL), norm_g.reshape(1, D_MODEL), s0, tables)
    return o, s_fin


def _select_bias(gate, n_valid_blocks, axis):
    nb = gate.shape[axis]
    blk = lax.broadcasted_iota(jnp.int32, gate.shape, axis)
    valid = blk < n_valid_blocks
    beaten = jnp.zeros(gate.shape, jnp.int32)
    for m in range(min(nb, n_valid_blocks)):
        gm = gate[m:m + 1, :] if axis == 0 else gate[:, m:m + 1]
        wins = (gm > gate) | ((gm == gate) & (m < blk))
        beaten = beaten + jnp.where(wins, 1, 0)
    return jnp.where(valid & (beaten < MOBA_TOPK), 0.0, NEG_INF)


def _moba_seq_body(q_ref, k_ref, v_ref, o_ref, kb_ref, vt_ref, kmean_ref, *, nb, heads):
    i = pl.program_id(2)
    hs = range(heads)
    cols = [slice(h * HEAD, (h + 1) * HEAD) for h in hs]

    @pl.when(i == 0)
    def _():
        for h in hs:
            kf = k_ref[:, cols[h]]
            kb_ref[h] = kf.astype(BF16)
            vt_ref[h] = v_ref[:, cols[h]].T.astype(BF16)
            means = [jnp.mean(kf[n * MOBA_BLOCK:(n + 1) * MOBA_BLOCK], axis=0, keepdims=True)
                     for n in range(nb)]
            kmean_ref[h] = jnp.concatenate(means, axis=0)

    qt = [q_ref[:, cols[h]].T for h in hs]
    qtb = [a.astype(BF16) for a in qt]
    gate = [_dot(kmean_ref[h], qt[h], precision=lax.Precision.HIGHEST) for h in hs]
    key = lax.broadcasted_iota(jnp.int32, (MOBA_BLOCK, MOBA_BLOCK), 0)
    qry = lax.broadcasted_iota(jnp.int32, (MOBA_BLOCK, MOBA_BLOCK), 1)

    def attend(own):
        n_keys = (own + 1) * MOBA_BLOCK
        bias = [_select_bias(gate[h], own, axis=0) for h in hs]
        s_all = [_dot(kb_ref[h, 0:n_keys, :], qtb[h]) for h in hs]
        s, m = [], []
        for h in hs:
            sh = [s_all[h][n * MOBA_BLOCK:(n + 1) * MOBA_BLOCK] + bias[h][n:n + 1]
                  for n in range(own)]
            sh.append(jnp.where(key <= qry, s_all[h][own * MOBA_BLOCK:n_keys], NEG_INF))
            mh = jnp.max(sh[own], axis=0, keepdims=True)
            for n in range(own):
                mh = jnp.maximum(mh, jnp.max(sh[n], axis=0, keepdims=True))
            s.append(sh)
            m.append(mh)
        for h in hs:
            l = jnp.zeros_like(m[h])
            ps = []
            for n in range(own + 1):
                p = jnp.exp(s[h][n] - m[h])
                l = l + jnp.sum(p, axis=0, keepdims=True)
                ps.append(p.astype(BF16))
            acc = _dot(vt_ref[h, :, 0:n_keys], jnp.concatenate(ps, axis=0))
            o_ref[:, cols[h]] = (acc / l).T.astype(o_ref.dtype)

    for own in range(nb):
        pl.when(i == own)(functools.partial(attend, own))


def _moba_seq(q, k, v, *, batch, seq):
    nb = seq // MOBA_BLOCK
    heads = MOBA_HEADS_PER_STEP
    width = heads * HEAD
    return pl.pallas_call(
        functools.partial(_moba_seq_body, nb=nb, heads=heads),
        grid=(batch, N_HEADS // heads, nb),
        in_specs=[pl.BlockSpec((MOBA_BLOCK, width), lambda b, h, i: (b * nb + i, h)),
                  pl.BlockSpec((seq, width), lambda b, h, i: (b, h)),
                  pl.BlockSpec((seq, width), lambda b, h, i: (b, h))],
        out_specs=pl.BlockSpec((MOBA_BLOCK, width), lambda b, h, i: (b * nb + i, h)),
        out_shape=jax.ShapeDtypeStruct((batch * seq, D_MODEL), BF16),
        scratch_shapes=[pltpu.VMEM((heads, seq, HEAD), BF16), pltpu.VMEM((heads, HEAD, seq), BF16),
                        pltpu.VMEM((heads, nb, HEAD), F32)],
        compiler_params=_params(),
        name="moba_prompt",
    )(q, k, v)


def _moba_paged_body(pt_ref, q_ref, kn_ref, vn_ref, *refs, nb, tq):
    del pt_ref
    n_page_refs = PAGED_BLOCKS_PER_STEP * PAGES_PER_BLOCK
    k_refs, v_refs = refs[:n_page_refs], refs[n_page_refs:2 * n_page_refs]
    o_ref, qs_ref, hmask_ref, kmean_ref, m_ref, l_ref, part_ref = refs[2 * n_page_refs:]
    step = pl.program_id(1)
    n_steps = nb // PAGED_BLOCKS_PER_STEP
    rows = N_HEADS * tq
    page_rows = PAGE_SIZE * N_HEADS
    head_shift = N_HEADS.bit_length() - 1
    tq_shift = tq.bit_length() - 1

    @pl.when(step == 0)
    def _():
        q = q_ref[...]
        qs_ref[...] = jnp.concatenate([q[:, h * HEAD:(h + 1) * HEAD] for h in range(N_HEADS)],
                                      axis=0)
        r = lax.broadcasted_iota(jnp.int32, (rows, page_rows), 0)
        cc = lax.broadcasted_iota(jnp.int32, (rows, page_rows), 1)
        hmask_ref[...] = jnp.where((cc & (N_HEADS - 1)) == (r >> tq_shift), 0.0, NEG_INF)
        m_ref[...] = jnp.zeros_like(m_ref)
        l_ref[...] = jnp.zeros_like(l_ref)

    qsb = qs_ref[...].astype(BF16)
    hmask = hmask_ref[...]
    lane = lax.broadcasted_iota(jnp.int32, (rows, LANES), 1)
    ones = jnp.ones((page_rows, HEAD), BF16)

    for bi in range(PAGED_BLOCKS_PER_STEP):
        n = step * PAGED_BLOCKS_PER_STEP + bi
        k_pages = k_refs[bi * PAGES_PER_BLOCK:(bi + 1) * PAGES_PER_BLOCK]
        v_pages = v_refs[bi * PAGES_PER_BLOCK:(bi + 1) * PAGES_PER_BLOCK]

        ksum = jnp.sum(k_pages[0][...], axis=0)
        for kp in k_pages[1:]:
            ksum = ksum + jnp.sum(kp[...], axis=0)
        kmean_ref[n] = ksum * (1.0 / MOBA_BLOCK)

        for pg, (kp, vp) in enumerate(zip(k_pages, v_pages)):
            s = _dot_nt(qsb, kp[...].reshape(page_rows, HEAD).astype(BF16)) + hmask
            tiles = [s[:, c * LANES:(c + 1) * LANES] for c in range(page_rows // LANES)]
            m_n = jnp.max(functools.reduce(jnp.maximum, tiles), axis=-1, keepdims=True)
            p = jnp.exp(s - m_n).astype(BF16)
            v_aug = jnp.concatenate([vp[...].reshape(page_rows, HEAD).astype(BF16), ones], axis=1)
            o_aug = _dot(p, v_aug)
            part_ref[pg, n] = o_aug[:, 0:HEAD]
            m_ref[pg] = jnp.where(lane == n, m_n, m_ref[pg])
            l_ref[pg] = jnp.where(lane == n, o_aug[:, HEAD:HEAD + 1], l_ref[pg])

    @pl.when(step == n_steps - 1)
    def _():
        qs = qs_ref[...]
        gates = [_dot_nt(qs[h * tq:(h + 1) * tq], kmean_ref[:, h, :], precision=lax.Precision.HIGHEST)
                 for h in range(N_HEADS)]
        bias = _select_bias(jnp.concatenate(gates, axis=0), nb, axis=1)

        s_own = _dot_nt(qsb, kn_ref[...].astype(BF16))
        r = lax.broadcasted_iota(jnp.int32, s_own.shape, 0)
        cc = lax.broadcasted_iota(jnp.int32, s_own.shape, 1)
        keep = ((cc & (N_HEADS - 1)) == (r >> tq_shift)) & ((cc >> head_shift) <= (r & (tq - 1)))
        s_own = jnp.where(keep, s_own, NEG_INF)
        m_own = jnp.max(s_own, axis=-1, keepdims=True)
        p_own = jnp.exp(s_own - m_own)
        l_own = jnp.sum(p_own, axis=-1, keepdims=True)
        o_own = _dot(p_own.astype(BF16), vn_ref[...].astype(BF16))

        mb = [m_ref[pg][:, 0:nb] + bias for pg in range(PAGES_PER_BLOCK)]
        m = m_own
        for mbp in mb:
            m = jnp.maximum(m, jnp.max(mbp, axis=-1, keepdims=True))
        w_own = jnp.exp(m_own - m)
        l = w_own * l_own
        o = w_own * o_own
        for pg in range(PAGES_PER_BLOCK):
            w = jnp.exp(mb[pg] - m)
            l = l + jnp.sum(w * l_ref[pg][:, 0:nb], axis=-1, keepdims=True)
            for j in range(nb):
                o = o + w[:, j:j + 1] * part_ref[pg, j]
        o = o / l
        for h in range(N_HEADS):
            o_ref[:, h * HEAD:(h + 1) * HEAD] = o[h * tq:(h + 1) * tq]


def _moba_paged(q, k_new, v_new, cache_k, cache_v, page_table, *, batch, tq):
    n_pages = page_table.shape[1]
    nb = n_pages // PAGES_PER_BLOCK
    rows = N_HEADS * tq
    assert nb <= LANES and tq & (tq - 1) == 0 and rows % 8 == 0
    assert nb % PAGED_BLOCKS_PER_STEP == 0
    page_block = (None, PAGE_SIZE, N_HEADS, HEAD)
    pages_per_step = PAGED_BLOCKS_PER_STEP * PAGES_PER_BLOCK

    def page(p):
        return pl.BlockSpec(page_block, lambda b, n, pt: (pt[b, pages_per_step * n + p], 0, 0, 0))

    pages = [page(p) for p in range(pages_per_step)]
    q_spec = pl.BlockSpec((tq, D_MODEL), lambda b, n, pt: (b, 0))
    new_spec = pl.BlockSpec((rows, HEAD), lambda b, n, pt: (b, 0))
    grid_spec = pltpu.PrefetchScalarGridSpec(
        num_scalar_prefetch=1,
        grid=(batch, nb // PAGED_BLOCKS_PER_STEP),
        in_specs=[q_spec, new_spec, new_spec] + pages + pages,
        out_specs=q_spec,
        scratch_shapes=[pltpu.VMEM((rows, HEAD), F32),
                        pltpu.VMEM((rows, PAGE_SIZE * N_HEADS), F32),
                        pltpu.VMEM((nb, N_HEADS, HEAD), F32),
                        pltpu.VMEM((PAGES_PER_BLOCK, rows, LANES), F32),
                        pltpu.VMEM((PAGES_PER_BLOCK, rows, LANES), F32),
                        pltpu.VMEM((PAGES_PER_BLOCK, nb, rows, HEAD), F32)],
    )
    k_rows = k_new.reshape(batch * rows, HEAD)
    v_rows = v_new.reshape(batch * rows, HEAD)
    return pl.pallas_call(
        functools.partial(_moba_paged_body, nb=nb, tq=tq),
        grid_spec=grid_spec,
        out_shape=jax.ShapeDtypeStruct((batch * tq, D_MODEL), F32),
        compiler_params=_params(),
        name="moba_paged",
    )(page_table, q, k_rows, v_rows, *([cache_k] * pages_per_step), *([cache_v] * pages_per_step))


TN_WIDE = 1024
TN_PAIR = 512


def _tile_sizes(m):
    if m >= 2048:
        return dict(tm_wide=1024, tm_pair=2048, tm_ln=512)
    return dict(tm_wide=m, tm_pair=m, tm_ln=m)


def _trunk(x, mod, mod_kv, s0, attend, weights, *, batch, seq):
    (a_w_in, lb_all, a_norm_g, a_w_out, b_w_q, b_w_o, w_kv, ln_g, ln_b, ffn_w_in, ffn_w_out) = weights
    m = batch * seq
    x = x.reshape(m, D_MODEL)
    ts = _tile_sizes(m)
    tm_wide, tm_ln = ts["tm_wide"], ts["tm_ln"]
    per_row = seq < tm_ln
    rows_per_group = m if per_row else seq

    def vecs(mvec, n):
        out = []
        for part in jnp.split(mvec, n, axis=-1):
            if per_row:
                out.append(jnp.repeat(part, seq, axis=0).reshape(1, m, D_MODEL))
            else:
                out.append(part.reshape(batch, 1, D_MODEL))
        return out

    sh1, sc1, g1, sh2, sc2, g2 = vecs(mod[0], 6)
    sh1b, sc1b, g1b, sh2b, sc2b, g2b = vecs(mod[1], 6)
    shift_kv, scale_kv = vecs(mod_kv, 2)
    common = dict(tm=tm_ln, rows_per_group=rows_per_group)
    narrow = dict(tm_mm=tm_wide, tn=TN_WIDE, **common)
    deep = dict(tm_mm=tm_ln, tn=TN_PAIR, **common)

    h = _modulate(x, sc1, sh1, tm=tm_ln, rows_per_group=rows_per_group)
    proj = _matmul(h, a_w_in, tm=tm_wide, tn=TN_WIDE, name="hgrn_in_proj")
    o, s_fin = _hgrn(proj, lb_all[0], a_norm_g[0], s0[0], batch=batch, seq=seq,
                     out_dtype=BF16 if seq >= HEAD else F32)
    x, (h,) = _matmul_ln(o, a_w_out, x, g1, ln_g[0, 0], ln_b[0, 0], [(sc2, sh2)],
                         name="hgrn_out_proj", **narrow)
    hid = _matmul_pair(h, ffn_w_in, layer=0, swiglu=True, tm=ts["tm_pair"], tn=TN_PAIR, name="swiglu_in")
    x, (h, z) = _matmul_ln(hid, ffn_w_out, x, g2, ln_g[0, 1], ln_b[0, 1],
                           [(sc1b, sh1b), (scale_kv, shift_kv)], layer=0,
                           name="ffn0_out_proj", **deep)

    k, v = _matmul_pair(z, w_kv, swiglu=False, tm=tm_wide, tn=TN_PAIR, name="kv_proj")
    q = _matmul(h, b_w_q, scale=HEAD ** -0.5, tm=tm_wide, tn=TN_WIDE, name="q_proj")
    o = attend(q, k, v)
    x, (h,) = _matmul_ln(o, b_w_o, x, g1b, ln_g[1, 0], ln_b[1, 0], [(sc2b, sh2b)],
                         name="moba_out_proj", **narrow)
    hid = _matmul_pair(h, ffn_w_in, layer=1, swiglu=True, tm=ts["tm_pair"], tn=TN_PAIR, name="swiglu_in")
    y, _ = _matmul_ln(hid, ffn_w_out, x, g2b, ln_g[1, 1], ln_b[1, 1], [], layer=1,
                      name="ffn1_out_proj", **deep)

    kv_shape = (batch, seq, N_HEADS, HEAD)
    return y.reshape(batch, seq, D_MODEL), s_fin[None], k.reshape(kv_shape), v.reshape(kv_shape)


def kernel(x_prompt, x_sample, state_hgrn, cache_k, cache_v, page_table, c_prompt, c_sample,
           a_w_in, a_lb_logits, a_norm_g, a_w_out, b_w_q, b_w_o, w_ada_kv, b_ada_kv, w_kv,
           w_ada, b_ada, ln_g, ln_b, ffn_w_in, ffn_w_out):
    assert a_w_in.shape[0] == 1 and b_w_q.shape[0] == 1 and w_ada.shape[0] == DEPTH
    bp, tp, _ = x_prompt.shape
    bs, tsamp, _ = x_sample.shape

    lb_all = _lower_bounds(a_lb_logits.astype(F32))

    c_all = jnp.concatenate([c_prompt, c_sample], axis=0)
    n_c = c_all.shape[0]
    c_rows = -(-n_c // BF16_ROWS) * BF16_ROWS
    c_all = jnp.pad(c_all, ((0, c_rows - n_c), (0, 0)))
    ada = functools.partial(_matmul, silu_in=True, tm=c_rows, tn=TN_WIDE)
    mod = [ada(c_all, w_ada, layer=l, bias=b_ada[l], name="ada_layer") for l in range(DEPTH)]
    mod_kv = ada(c_all, w_ada_kv, bias=b_ada_kv, name="ada_kv")

    weights = (a_w_in, lb_all, a_norm_g, a_w_out, b_w_q, b_w_o, w_kv, ln_g, ln_b, ffn_w_in, ffn_w_out)

    s0_prompt = jnp.zeros((a_w_in.shape[0], bp, N_HEADS, HEAD, HEAD), state_hgrn.dtype)
    y_p, s_p, k_p, v_p = _trunk(
        x_prompt, [mm[:bp] for mm in mod], mod_kv[:bp], s0_prompt,
        functools.partial(_moba_seq, batch=bp, seq=tp), weights, batch=bp, seq=tp)
    y_s, s_s, k_s, v_s = _trunk(
        x_sample, [mm[bp:bp + bs] for mm in mod], mod_kv[bp:bp + bs], state_hgrn,
        lambda q, k, v: _moba_paged(q, k, v, cache_k, cache_v, page_table, batch=bs, tq=tsamp),
        weights, batch=bs, seq=tsamp)
    return (y_p, y_s, s_p, s_s, k_p, v_p, k_s, v_s)
```
